```python
import math
import jax
import jax.numpy as jnp
from jax import lax
import numpy as np

D_MODEL = 2048
BATCH = 2
SEQ = 16384
DEPTH = 2

D_MIX = D_MODEL
HEAD_DIM = 64
ATTN_WIDTH = D_MIX // 2
N_Q_HEADS = ATTN_WIDTH // HEAD_DIM
N_KV_HEADS = max(1, N_Q_HEADS // 8)
KV_WIDTH = N_KV_HEADS * HEAD_DIM
WINDOW = 128
ATTN_BLOCK = 128

SSM_WIDTH = D_MIX // 4
SSM_GROUP = 16
SSM_GROUPS = SSM_WIDTH // SSM_GROUP
SSM_STATE = 64
DT_MIN = 1e-3
DT_MAX = 1e-1

GMLP_WIDTH = D_MIX - ATTN_WIDTH - SSM_WIDTH
GMLP_CHUNK = 128
GMLP_HEAD = 128
GMLP_GROUPS = GMLP_WIDTH // GMLP_HEAD

D_IN_PROJ = ATTN_WIDTH + 2 * KV_WIDTH + SSM_WIDTH + 2 * GMLP_WIDTH
D_FF = -(-8 * D_MODEL // (3 * 256)) * 256
EPS = 1e-5

kernel_name = "hybrid_parallel_heads_s5_gmlp_swa"


def rmsnorm(x, g):
    xf = x.astype(jnp.float32)
    y = xf * lax.rsqrt(jnp.mean(xf * xf, axis=-1, keepdims=True) + EPS)
    return (y * g.astype(jnp.float32)).astype(x.dtype)


def layernorm(x, g, b):
    xf = x.astype(jnp.float32)
    xc = xf - jnp.mean(xf, axis=-1, keepdims=True)
    y = xc * lax.rsqrt(jnp.mean(xc * xc, axis=-1, keepdims=True) + EPS)
    return (y * g.astype(jnp.float32) + b.astype(jnp.float32)).astype(x.dtype)


def sliding_window_gqa(q, k, v, sinks):
    bsz, L, _ = q.shape
    nb = L // ATTN_BLOCK
    grp = N_Q_HEADS // N_KV_HEADS
    qb = q.reshape(bsz, nb, ATTN_BLOCK, N_KV_HEADS, grp, HEAD_DIM)

    def band(t):
        tb = t.reshape(bsz, nb, ATTN_BLOCK, N_KV_HEADS, HEAD_DIM)
        prev = jnp.concatenate([jnp.zeros_like(tb[:, :1]), tb[:, :-1]], axis=1)
        return jnp.concatenate([prev, tb], axis=2)

    kb, vb = band(k), band(v)
    scores = jnp.einsum("bnqkgd,bnskd->bnkgqs", qb, kb).astype(jnp.float32) * (HEAD_DIM ** -0.5)
    q_loc = jnp.arange(ATTN_BLOCK)[:, None] + ATTN_BLOCK
    k_loc = jnp.arange(2 * ATTN_BLOCK)[None, :]
    rel = q_loc - k_loc
    in_window = (rel >= 0) & (rel < WINDOW)
    k_abs = jnp.arange(nb)[:, None] * ATTN_BLOCK - ATTN_BLOCK + k_loc
    mask = in_window[None] & (k_abs >= 0)[:, None, :]
    scores = jnp.where(mask[None, :, None, None], scores, -jnp.inf)
    sink = sinks.astype(jnp.float32).reshape(N_KV_HEADS, grp)[None, None, :, :, None, None]
    sink = jnp.broadcast_to(sink, scores.shape[:-1] + (1,))
    probs = jax.nn.softmax(jnp.concatenate([scores, sink], axis=-1), axis=-1)[..., :-1]
    out = jnp.einsum("bnkgqs,bnskd->bnqkgd", probs.astype(v.dtype), vb)
    return out.reshape(bsz, L, ATTN_WIDTH)


def _ssm_combine(c1, c2):
    a1r, a1i, b1r, b1i = c1
    a2r, a2i, b2r, b2i = c2
    return (a2r * a1r - a2i * a1i,
            a2r * a1i + a2i * a1r,
            a2r * b1r - a2i * b1i + b2r,
            a2r * b1i + a2i * b1r + b2i)


def s5_mixer(u, lam_re, lam_im, log_dt, b_re, b_im, c_re, c_im, d_skip, w_glu):
    f32 = jnp.float32
    bsz, L, _ = u.shape
    uf = u.astype(f32)
    ug = uf.reshape(bsz, L, SSM_GROUPS, SSM_GROUP)
    lr, li = lam_re.astype(f32), lam_im.astype(f32)
    dt = jnp.exp(log_dt.astype(f32))[:, None]
    mag = jnp.exp(lr * dt)
    abar_re, abar_im = mag * jnp.cos(li * dt), mag * jnp.sin(li * dt)
    den = lr * lr + li * li
    num_re, num_im = abar_re - 1.0, abar_im
    coef_re = (num_re * lr + num_im * li) / den
    coef_im = (num_im * lr - num_re * li) / den
    br, bi = b_re.astype(f32), b_im.astype(f32)
    bbar_re = coef_re[..., None] * br - coef_im[..., None] * bi
    bbar_im = coef_re[..., None] * bi + coef_im[..., None] * br
    bu_re = jnp.einsum("blgh,gph->lbgp", ug, bbar_re)
    bu_im = jnp.einsum("blgh,gph->lbgp", ug, bbar_im)
    a_re = jnp.broadcast_to(abar_re[None, None], (L, 1, SSM_GROUPS, SSM_STATE))
    a_im = jnp.broadcast_to(abar_im[None, None], (L, 1, SSM_GROUPS, SSM_STATE))
    _, _, s_re, s_im = lax.associative_scan(_ssm_combine, (a_re, a_im, bu_re, bu_im), axis=0)
    y = (jnp.einsum("lbgp,ghp->blgh", s_re, c_re.astype(f32))
         - jnp.einsum("lbgp,ghp->blgh", s_im, c_im.astype(f32)))
    y = y.reshape(bsz, L, SSM_WIDTH) + d_skip.astype(f32) * uf
    y = jax.nn.gelu(y).astype(u.dtype)
    return y * jax.nn.sigmoid(y @ w_glu)


def gmlp_mixer(zu, zv, ln_g, ln_b, w_s, b_s):
    bsz, L, _ = zu.shape
    nc = L // GMLP_CHUNK
    u = jax.nn.gelu(zu)
    v = layernorm(jax.nn.gelu(zv), ln_g, ln_b)
    vc = v.reshape(bsz, nc, GMLP_CHUNK, GMLP_GROUPS, GMLP_HEAD)
    causal = jnp.tril(jnp.ones((GMLP_CHUNK, GMLP_CHUNK), dtype=bool))
    ws = jnp.where(causal[None], w_s, jnp.zeros_like(w_s))
    mixed = jnp.einsum("gts,bcsgh->bctgh", ws, vc) + jnp.swapaxes(b_s, 0, 1)[None, None, :, :, None]
    return u * mixed.reshape(bsz, L, GMLP_WIDTH)


def setup_inputs(seed: int = 0) -> dict:
    key = jax.random.key(seed)
    ks = jax.random.split(key, 26)
    f32 = jnp.float32

    def nrm(k, shape, scale):
        return jax.random.normal(k, shape, f32) * scale

    def gain(k, shape):
        return 1.0 + 0.02 * jax.random.normal(k, shape, f32)

    n_idx = jnp.arange(SSM_STATE, dtype=f32)
    gp = (DEPTH, SSM_GROUPS, SSM_STATE)
    return {
        "x": nrm(ks[0], (BATCH, SEQ, D_MODEL), 1.0),
        "norm_mix": gain(ks[1], (DEPTH, D_MODEL)),
        "w_in": nrm(ks[2], (DEPTH, D_MODEL, D_IN_PROJ), D_MODEL ** -0.5),
        "attn_sinks": nrm(ks[3], (DEPTH, N_Q_HEADS), 0.5),
        "ssm_lam_re": -0.5 + nrm(ks[4], gp, 0.01),
        "ssm_lam_im": math.pi * n_idx + nrm(ks[5], gp, 0.01),
        "ssm_log_dt": jax.random.uniform(ks[6], (DEPTH, SSM_GROUPS), f32,
                                         math.log(DT_MIN), math.log(DT_MAX)),
        "ssm_b_re": nrm(ks[7], (DEPTH, SSM_GROUPS, SSM_STATE, SSM_GROUP), (2 * SSM_GROUP) ** -0.5),
        "ssm_b_im": nrm(ks[8], (DEPTH, SSM_GROUPS, SSM_STATE, SSM_GROUP), (2 * SSM_GROUP) ** -0.5),
        "ssm_c_re": nrm(ks[9], (DEPTH, SSM_GROUPS, SSM_GROUP, SSM_STATE), (2 * SSM_STATE) ** -0.5),
        "ssm_c_im": nrm(ks[10], (DEPTH, SSM_GROUPS, SSM_GROUP, SSM_STATE), (2 * SSM_STATE) ** -0.5),
        "ssm_d": nrm(ks[11], (DEPTH, SSM_WIDTH), 1.0),
        "ssm_w_glu": nrm(ks[12], (DEPTH, SSM_WIDTH, SSM_WIDTH), SSM_WIDTH ** -0.5),
        "gmlp_ln_g": gain(ks[13], (DEPTH, GMLP_WIDTH)),
        "gmlp_ln_b": nrm(ks[14], (DEPTH, GMLP_WIDTH), 0.02),
        "gmlp_w_s": nrm(ks[15], (DEPTH, GMLP_GROUPS, GMLP_CHUNK, GMLP_CHUNK), GMLP_CHUNK ** -0.5),
        "gmlp_b_s": 1.0 + nrm(ks[16], (DEPTH, GMLP_GROUPS, GMLP_CHUNK), 0.1),
        "out_norm_attn": gain(ks[17], (DEPTH, ATTN_WIDTH)),
        "out_norm_ssm": gain(ks[18], (DEPTH, SSM_WIDTH)),
        "out_norm_gmlp": gain(ks[19], (DEPTH, GMLP_WIDTH)),
        "w_out": nrm(ks[20], (DEPTH, D_MIX, D_MODEL), D_MIX ** -0.5),
        "norm_ffn": gain(ks[21], (DEPTH, D_MODEL)),
        "w_gate": nrm(ks[22], (DEPTH, D_MODEL, D_FF), D_MODEL ** -0.5),
        "w_up": nrm(ks[23], (DEPTH, D_MODEL, D_FF), D_MODEL ** -0.5),
        "w_down": nrm(ks[24], (DEPTH, D_FF, D_MODEL), D_FF ** -0.5),
        "norm_final": gain(ks[25], (D_MODEL,)),
    }


def reference(x, norm_mix, w_in, attn_sinks, ssm_lam_re, ssm_lam_im, ssm_log_dt,
              ssm_b_re, ssm_b_im, ssm_c_re, ssm_c_im, ssm_d, ssm_w_glu,
              gmlp_ln_g, gmlp_ln_b, gmlp_w_s, gmlp_b_s,
              out_norm_attn, out_norm_ssm, out_norm_gmlp, w_out,
              norm_ffn, w_gate, w_up, w_down, norm_final):
    splits = [ATTN_WIDTH,
              ATTN_WIDTH + KV_WIDTH,
              ATTN_WIDTH + 2 * KV_WIDTH,
              ATTN_WIDTH + 2 * KV_WIDTH + SSM_WIDTH,
              ATTN_WIDTH + 2 * KV_WIDTH + SSM_WIDTH + GMLP_WIDTH]
    for l in range(DEPTH):
        h = rmsnorm(x, norm_mix[l])
        z = h @ w_in[l]
        q, k, v, u_ssm, z_u, z_v = jnp.split(z, splits, axis=-1)
        y_attn = sliding_window_gqa(q, k, v, attn_sinks[l])
        y_ssm = s5_mixer(u_ssm, ssm_lam_re[l], ssm_lam_im[l], ssm_log_dt[l],
                         ssm_b_re[l], ssm_b_im[l], ssm_c_re[l], ssm_c_im[l],
                         ssm_d[l], ssm_w_glu[l])
        y_gmlp = gmlp_mixer(z_u, z_v, gmlp_ln_g[l], gmlp_ln_b[l], gmlp_w_s[l], gmlp_b_s[l])
        y = jnp.concatenate([rmsnorm(y_attn, out_norm_attn[l]),
                             rmsnorm(y_ssm, out_norm_ssm[l]),
                             rmsnorm(y_gmlp, out_norm_gmlp[l])], axis=-1)
        x = x + y @ w_out[l]
        h = rmsnorm(x, norm_ffn[l])
        x = x + (jax.nn.silu(h @ w_gate[l]) * (h @ w_up[l])) @ w_down[l]
    return rmsnorm(x, norm_final)
```

```python
import functools
import math

import jax
import jax.numpy as jnp
from jax import lax
from jax.experimental import pallas as pl
from jax.experimental.pallas import tpu as pltpu

F32 = jnp.float32
BF16 = jnp.bfloat16

D_MODEL = 2048
HEAD_DIM = 64
ATTN_WIDTH = 1024
N_Q_HEADS = ATTN_WIDTH // HEAD_DIM
N_KV_HEADS = 2
KV_WIDTH = N_KV_HEADS * HEAD_DIM
ATTN_BLOCK = 128
SSM_WIDTH = 512
SSM_GROUP = 16
SSM_GROUPS = SSM_WIDTH // SSM_GROUP
SSM_STATE = 64
GMLP_WIDTH = 512
GMLP_CHUNK = 128
GMLP_GROUPS = GMLP_WIDTH // 128
EPS = 1e-5

LANES = 128
SSM_CHUNK = LANES
SSM_SCAN_STEPS = 7
VMEM_LIMIT = 56 * 1024 * 1024


def _gelu_tanh(x):
    c = math.sqrt(2.0 / math.pi)
    return 0.5 * x * (1.0 + jnp.tanh(c * (x + 0.044715 * (x * x * x))))


def _rms_lanes(y, gain):
    ms = jnp.mean(y * y, axis=-1, keepdims=True)
    return y * lax.rsqrt(ms + EPS) * gain


def _in_proj_kernel(x_ref, g_ref, wq_ref, wkv_ref, wzu_ref, wzv_ref, wst_ref,
                    q_ref, kv_ref, zu_ref, zv_ref, ut_ref):
    h = _rms_lanes(x_ref[...], g_ref[...]).astype(BF16)
    q_ref[...] = (jnp.dot(h, wq_ref[...], preferred_element_type=F32) * (HEAD_DIM ** -0.5)).astype(BF16)
    kv_ref[...] = jnp.dot(h, wkv_ref[...], preferred_element_type=F32).astype(BF16)
    zu_ref[...] = jnp.dot(h, wzu_ref[...], preferred_element_type=F32).astype(BF16)
    zv_ref[...] = jnp.dot(h, wzv_ref[...], preferred_element_type=F32).astype(BF16)
    ut = lax.dot_general(wst_ref[...], h, (((1,), (1,)), ((), ())), preferred_element_type=F32)
    ut_ref[...] = ut.astype(BF16)


def _in_proj(x2, gain, wq, wkv, wzu, wzv, wst, tm):
    t = x2.shape[0]
    const = lambda i: (0, 0)
    row = lambda i: (i, 0)
    return pl.pallas_call(
        _in_proj_kernel,
        grid=(t // tm,),
        in_specs=[
            pl.BlockSpec((tm, D_MODEL), row),
            pl.BlockSpec((1, D_MODEL), const),
            pl.BlockSpec(wq.shape, const),
            pl.BlockSpec(wkv.shape, const),
            pl.BlockSpec(wzu.shape, const),
            pl.BlockSpec(wzv.shape, const),
            pl.BlockSpec(wst.shape, const),
        ],
        out_specs=[
            pl.BlockSpec((tm, ATTN_WIDTH), row),
            pl.BlockSpec((tm, 2 * KV_WIDTH), row),
            pl.BlockSpec((tm, GMLP_WIDTH), row),
            pl.BlockSpec((tm, GMLP_WIDTH), row),
            pl.BlockSpec((SSM_WIDTH, tm), lambda i: (0, i)),
        ],
        out_shape=[
            jax.ShapeDtypeStruct((t, ATTN_WIDTH), BF16),
            jax.ShapeDtypeStruct((t, 2 * KV_WIDTH), BF16),
            jax.ShapeDtypeStruct((t, GMLP_WIDTH), BF16),
            jax.ShapeDtypeStruct((t, GMLP_WIDTH), BF16),
            jax.ShapeDtypeStruct((SSM_WIDTH, t), BF16),
        ],
        compiler_params=pltpu.CompilerParams(
            dimension_semantics=("parallel",), vmem_limit_bytes=VMEM_LIMIT),
        name="in_proj",
    )(x2, gain, wq, wkv, wzu, wzv, wst)


def _attn_kernel(q_ref, kvp_ref, kvc_ref, sink_ref, g_ref, o_ref):
    n = pl.program_id(1)
    q = q_ref[...]
    kv = jnp.concatenate([kvp_ref[...], kvc_ref[...]], axis=0).astype(F32)
    kk = kv[:, :LANES]
    vv = kv[:, LANES:]
    kk_sw = pltpu.roll(kk, HEAD_DIM, axis=1)
    vv_sw = pltpu.roll(vv, HEAD_DIM, axis=1)
    lo = lax.broadcasted_iota(jnp.int32, kk.shape, 1) < HEAD_DIM

    rows = 4 * ATTN_BLOCK
    qi = lax.broadcasted_iota(jnp.int32, (rows, 2 * ATTN_BLOCK), 0) & (ATTN_BLOCK - 1)
    ks = lax.broadcasted_iota(jnp.int32, (rows, 2 * ATTN_BLOCK), 1)
    mask = (ks > qi) & (ks <= qi + ATTN_BLOCK) & ((n > 0) | (ks >= ATTN_BLOCK))
    sinks = sink_ref[...]

    outs = []
    for kh in range(N_KV_HEADS):
        if kh == 0:
            k_e, k_o = jnp.where(lo, kk, 0.0), jnp.where(lo, 0.0, kk_sw)
            v_e, v_o = jnp.where(lo, vv, 0.0), jnp.where(lo, 0.0, vv_sw)
        else:
            k_e, k_o = jnp.where(lo, kk_sw, 0.0), jnp.where(lo, 0.0, kk)
            v_e, v_o = jnp.where(lo, vv_sw, 0.0), jnp.where(lo, 0.0, vv)
        kz = jnp.concatenate([k_e, k_o], axis=0).astype(BF16)
        vz = jnp.concatenate([v_e, v_o], axis=0).astype(BF16)
        qs = jnp.concatenate(
            [q[:, (kh * 4 + p) * LANES:(kh * 4 + p + 1) * LANES] for p in range(4)], axis=0)
        s = lax.dot_general(qs, kz, (((1,), (1,)), ((), ())), preferred_element_type=F32)
        ps = []
        for par in range(2):
            sp = jnp.where(mask, s[:, par * 256:(par + 1) * 256], -jnp.inf)
            sk = sinks[:, kh * 2 + par:kh * 2 + par + 1]
            m = jnp.maximum(jnp.max(sp, axis=-1, keepdims=True), sk)
            e = jnp.exp(sp - m)
            den = jnp.sum(e, axis=-1, keepdims=True) + jnp.exp(sk - m)
            ps.append(e * (1.0 / den))
        p = jnp.concatenate(ps, axis=1).astype(BF16)
        o = jnp.dot(p, vz, preferred_element_type=F32)
        outs.extend([o[pp * ATTN_BLOCK:(pp + 1) * ATTN_BLOCK] for pp in range(4)])
    y = jnp.concatenate(outs, axis=1)
    o_ref[...] = _rms_lanes(y, g_ref[...]).astype(BF16)


def _attention(q, kv, sink_cols, gain, bsz, nb):
    t = q.shape[0]
    return pl.pallas_call(
        _attn_kernel,
        grid=(bsz, nb),
        in_specs=[
            pl.BlockSpec((ATTN_BLOCK, ATTN_WIDTH), lambda b, n: (b * nb + n, 0)),
            pl.BlockSpec((ATTN_BLOCK, 2 * KV_WIDTH), lambda b, n: (b * nb + jnp.maximum(n - 1, 0), 0)),
            pl.BlockSpec((ATTN_BLOCK, 2 * KV_WIDTH), lambda b, n: (b * nb + n, 0)),
            pl.BlockSpec(sink_cols.shape, lambda b, n: (0, 0)),
            pl.BlockSpec((1, ATTN_WIDTH), lambda b, n: (0, 0)),
        ],
        out_specs=pl.BlockSpec((ATTN_BLOCK, ATTN_WIDTH), lambda b, n: (b * nb + n, 0)),
        out_shape=jax.ShapeDtypeStruct((t, ATTN_WIDTH), BF16),
        compiler_params=pltpu.CompilerParams(
            dimension_semantics=("parallel", "parallel"), vmem_limit_bytes=VMEM_LIMIT),
        name="swa_attention",
    )(q, kv, kv, sink_cols, gain)


def _ssm_kernel(u_ref, kc_ref, e_ref, f_ref, ap_ref, y_ref, m_scr, *, chunks_per_seq):
    rows = u_ref.shape[2]
    tri = (lax.broadcasted_iota(jnp.int32, (LANES, LANES), 1)
           >= lax.broadcasted_iota(jnp.int32, (LANES, LANES), 0))

    def build(hp, carry):
        for h in range(SSM_GROUP):
            k = kc_ref[0, pl.ds(hp * SSM_GROUP + h, 1), :]
            kb = jnp.broadcast_to(k, (LANES, LANES))
            kr = pltpu.roll(kb, 0, 1, stride=1, stride_axis=0)
            blk = jnp.where(tri, kr, 0.0).astype(BF16)
            m_scr[pl.ds(pl.multiple_of(hp * LANES, LANES), LANES), h * LANES:(h + 1) * LANES] = blk
        return carry

    lax.fori_loop(0, SSM_GROUP, build, 0)

    u = jnp.concatenate([u_ref[0, hp] for hp in range(SSM_GROUP)], axis=1)
    y = jnp.dot(u, m_scr[...], preferred_element_type=F32)
    s = jnp.dot(u, e_ref[0], preferred_element_type=F32)

    cidx = lax.broadcasted_iota(jnp.int32, (rows, LANES), 0) & (chunks_per_seq - 1)
    for j in range(SSM_SCAN_STEPS):
        d = 1 << j
        prev = jnp.where(cidx >= d, pltpu.roll(s, d, axis=0), 0.0)
        a_same = ap_ref[0, 2 * j:2 * j + 1, :]
        a_cross = ap_ref[0, 2 * j + 1:2 * j + 2, :]
        s = s + prev * a_same + pltpu.roll(prev, SSM_STATE, axis=1) * a_cross
    s_prev = jnp.where(cidx >= 1, pltpu.roll(s, 1, axis=0), 0.0)
    y = y + jnp.dot(s_prev.astype(BF16), f_ref[0], preferred_element_type=F32)
    for h in range(SSM_GROUP):
        y_ref[0, h] = y[:, h * LANES:(h + 1) * LANES].astype(BF16)


def _ssm(u4, kc, e, f, ap, chunks_per_seq):
    g, hh, rows, _ = u4.shape
    return pl.pallas_call(
        functools.partial(_ssm_kernel, chunks_per_seq=chunks_per_seq),
        grid=(g,),
        in_specs=[
            pl.BlockSpec((1, hh, rows, LANES), lambda i: (i, 0, 0, 0)),
            pl.BlockSpec((1,) + kc.shape[1:], lambda i: (i, 0, 0)),
            pl.BlockSpec((1,) + e.shape[1:], lambda i: (i, 0, 0)),
            pl.BlockSpec((1,) + f.shape[1:], lambda i: (i, 0, 0)),
            pl.BlockSpec((1,) + ap.shape[1:], lambda i: (i, 0, 0)),
        ],
        out_specs=pl.BlockSpec((1, hh, rows, LANES), lambda i: (i, 0, 0, 0)),
        out_shape=jax.ShapeDtypeStruct(u4.shape, BF16),
        scratch_shapes=[pltpu.VMEM((SSM_GROUP * LANES, SSM_GROUP * LANES), BF16)],
        compiler_params=pltpu.CompilerParams(
            dimension_semantics=("parallel",), vmem_limit_bytes=VMEM_LIMIT),
        name="s5_chunked",
    )(u4, kc, e, f, ap)


def _ssm_tables(lam_re, lam_im, log_dt, b_re, b_im, c_re, c_im, d_skip):
    hp = lax.Precision.HIGHEST
    lr, li = lam_re.astype(F32), lam_im.astype(F32)
    dt = jnp.exp(log_dt.astype(F32))[:, None]
    mag = jnp.exp(lr * dt)
    ar, ai = mag * jnp.cos(li * dt), mag * jnp.sin(li * dt)
    den = lr * lr + li * li
    nre, nim = ar - 1.0, ai
    coef_re = (nre * lr + nim * li) / den
    coef_im = (nim * lr - nre * li) / den
    br, bi = b_re.astype(F32), b_im.astype(F32)
    bbr = coef_re[..., None] * br - coef_im[..., None] * bi
    bbi = coef_re[..., None] * bi + coef_im[..., None] * br
    cr, ci = c_re.astype(F32), c_im.astype(F32)
    n = jnp.arange(SSM_CHUNK + 1, dtype=F32)[:, None, None]
    pmag = jnp.exp(n * (lr * dt)[None])
    pr, pi = pmag * jnp.cos(n * (li * dt)[None]), pmag * jnp.sin(n * (li * dt)[None])
    crp = jnp.swapaxes(cr, 1, 2)
    cip = jnp.swapaxes(ci, 1, 2)
    wr = bbr[:, :, :, None] * crp[:, :, None, :] - bbi[:, :, :, None] * cip[:, :, None, :]
    wi = bbr[:, :, :, None] * cip[:, :, None, :] + bbi[:, :, :, None] * crp[:, :, None, :]
    taps = (jnp.einsum("ngp,gpab->gabn", pr[:SSM_CHUNK], wr, precision=hp)
            - jnp.einsum("ngp,gpab->gabn", pi[:SSM_CHUNK], wi, precision=hp))
    dskip = d_skip.astype(F32).reshape(SSM_GROUPS, SSM_GROUP)
    eye = jnp.eye(SSM_GROUP, dtype=F32)
    taps = taps.at[:, :, :, 0].add(eye[None] * dskip[:, None, :])
    kc = taps.reshape(SSM_GROUPS, SSM_GROUP * SSM_GROUP, SSM_CHUNK)
    prr = pr[:SSM_CHUNK][::-1]
    pir = pi[:SSM_CHUNK][::-1]
    bbr_t = jnp.swapaxes(bbr, 1, 2)
    bbi_t = jnp.swapaxes(bbi, 1, 2)
    prr_g = jnp.swapaxes(prr, 0, 1)[:, None]
    pir_g = jnp.swapaxes(pir, 0, 1)[:, None]
    e_re = prr_g * bbr_t[:, :, None, :] - pir_g * bbi_t[:, :, None, :]
    e_im = prr_g * bbi_t[:, :, None, :] + pir_g * bbr_t[:, :, None, :]
    e = jnp.concatenate([e_re, e_im], axis=-1).reshape(
        SSM_GROUPS, SSM_GROUP * SSM_CHUNK, 2 * SSM_STATE).astype(BF16)
    p1r = jnp.transpose(pr[1:], (1, 2, 0))[:, :, None, :]
    p1i = jnp.transpose(pi[1:], (1, 2, 0))[:, :, None, :]
    f_re = crp[:, :, :, None] * p1r - cip[:, :, :, None] * p1i
    f_im = crp[:, :, :, None] * p1i + cip[:, :, :, None] * p1r
    f = jnp.concatenate([f_re, -f_im], axis=1).reshape(
        SSM_GROUPS, 2 * SSM_STATE, SSM_GROUP * SSM_CHUNK).astype(BF16)
    a_r, a_i = pr[SSM_CHUNK], pi[SSM_CHUNK]
    rows = []
    for _ in range(SSM_SCAN_STEPS):
        rows.append(jnp.concatenate([a_r, a_r], axis=-1))
        rows.append(jnp.concatenate([-a_i, a_i], axis=-1))
        a_r, a_i = a_r * a_r - a_i * a_i, 2.0 * a_r * a_i
    rows.extend([jnp.zeros_like(rows[0])] * (16 - len(rows)))
    ap = jnp.stack(rows, axis=1)
    return kc, e, f, ap


def _glu_kernel(yt_ref, wt_ref, g_ref, o_ref):
    y = _gelu_tanh(yt_ref[...].astype(F32))
    z = jnp.dot(wt_ref[...], y.astype(BF16), preferred_element_type=F32)
    o = y * jax.nn.sigmoid(z)
    ms = jnp.mean(o * o, axis=0, keepdims=True)
    o = o * lax.rsqrt(ms + EPS) * g_ref[...]
    o_ref[...] = o.T.astype(BF16)


def _ssm_glu(yt, w_glu_t, gain_col, tm):
    t = yt.shape[1]
    return pl.pallas_call(
        _glu_kernel,
        grid=(t // tm,),
        in_specs=[
            pl.BlockSpec((SSM_WIDTH, tm), lambda i: (0, i)),
            pl.BlockSpec((SSM_WIDTH, SSM_WIDTH), lambda i: (0, 0)),
            pl.BlockSpec((SSM_WIDTH, 1), lambda i: (0, 0)),
        ],
        out_specs=pl.BlockSpec((tm, SSM_WIDTH), lambda i: (i, 0)),
        out_shape=jax.ShapeDtypeStruct((t, SSM_WIDTH), BF16),
        compiler_params=pltpu.CompilerParams(
            dimension_semantics=("parallel",), vmem_limit_bytes=VMEM_LIMIT),
        name="s5_glu",
    )(yt, w_glu_t, gain_col)


def _gmlp_kernel(zu_ref, zv_ref, lng_ref, lnb_ref, ws_ref, bs_ref, g_ref, o_ref):
    tm = zu_ref.shape[0]
    u = _gelu_tanh(zu_ref[...].astype(F32))
    v = _gelu_tanh(zv_ref[...].astype(F32))
    vc = v - jnp.mean(v, axis=-1, keepdims=True)
    v = vc * lax.rsqrt(jnp.mean(vc * vc, axis=-1, keepdims=True) + EPS) * lng_ref[...] + lnb_ref[...]
    vb = v.astype(BF16)
    bs = bs_ref[...]
    cols = []
    for g in range(GMLP_GROUPS):
        w = ws_ref[g]
        parts = []
        for c in range(tm // GMLP_CHUNK):
            vblk = vb[c * GMLP_CHUNK:(c + 1) * GMLP_CHUNK, g * LANES:(g + 1) * LANES]
            parts.append(jnp.dot(w, vblk, preferred_element_type=F32) + bs[:, g:g + 1])
        cols.append(jnp.concatenate(parts, axis=0))
    mixed = jnp.concatenate(cols, axis=1)
    o_ref[...] = _rms_lanes(u * mixed, g_ref[...]).astype(BF16)


def _gmlp(zu, zv, ln_g, ln_b, ws, bs_t, gain, tm):
    t = zu.shape[0]
    row = lambda i: (i, 0)
    const2 = lambda i: (0, 0)
    return pl.pallas_call(
        _gmlp_kernel,
        grid=(t // tm,),
        in_specs=[
            pl.BlockSpec((tm, GMLP_WIDTH), row),
            pl.BlockSpec((tm, GMLP_WIDTH), row),
            pl.BlockSpec((1, GMLP_WIDTH), const2),
            pl.BlockSpec((1, GMLP_WIDTH), const2),
            pl.BlockSpec(ws.shape, lambda i: (0, 0, 0)),
            pl.BlockSpec(bs_t.shape, const2),
            pl.BlockSpec((1, GMLP_WIDTH), const2),
        ],
        out_specs=pl.BlockSpec((tm, GMLP_WIDTH), row),
        out_shape=jax.ShapeDtypeStruct((t, GMLP_WIDTH), BF16),
        compiler_params=pltpu.CompilerParams(
            dimension_semantics=("parallel",), vmem_limit_bytes=VMEM_LIMIT),
        name="gmlp",
    )(zu, zv, ln_g, ln_b, ws, bs_t, gain)


def _out_proj_kernel(x_ref, ya_ref, ys_ref, yg_ref, w_ref, o_ref):
    y = jnp.concatenate([ya_ref[...], ys_ref[...], yg_ref[...]], axis=1)
    o_ref[...] = x_ref[...] + jnp.dot(y, w_ref[...], preferred_element_type=F32)


def _out_proj(x2, ya, ys, yg, w, tm):
    t = x2.shape[0]
    row = lambda i: (i, 0)
    return pl.pallas_call(
        _out_proj_kernel,
        grid=(t // tm,),
        in_specs=[
            pl.BlockSpec((tm, D_MODEL), row),
            pl.BlockSpec((tm, ATTN_WIDTH), row),
            pl.BlockSpec((tm, SSM_WIDTH), row),
            pl.BlockSpec((tm, GMLP_WIDTH), row),
            pl.BlockSpec(w.shape, lambda i: (0, 0)),
        ],
        out_specs=pl.BlockSpec((tm, D_MODEL), row),
        out_shape=jax.ShapeDtypeStruct((t, D_MODEL), F32),
        compiler_params=pltpu.CompilerParams(
            dimension_semantics=("parallel",), vmem_limit_bytes=VMEM_LIMIT),
        name="out_proj",
    )(x2, ya, ys, yg, w)


def _ffn_kernel(x_ref, g_ref, wg_ref, wu_ref, wd_ref, gf_ref, o_ref, h_scr, *, final_norm):
    j = pl.program_id(1)

    @pl.when(j == 0)
    def _():
        h_scr[...] = _rms_lanes(x_ref[...], g_ref[...]).astype(BF16)
        o_ref[...] = x_ref[...]

    h = h_scr[...]
    gate = jnp.dot(h, wg_ref[...], preferred_element_type=F32)
    up = jnp.dot(h, wu_ref[...], preferred_element_type=F32)
    a = (gate * jax.nn.sigmoid(gate) * up).astype(BF16)
    o_ref[...] += jnp.dot(a, wd_ref[...], preferred_element_type=F32)

    if final_norm:
        @pl.when(j == pl.num_programs(1) - 1)
        def _():
            o_ref[...] = _rms_lanes(o_ref[...], gf_ref[...])


def _ffn(x2, gain, wg, wu, wd, gain_final, tm, tf, final_norm):
    t = x2.shape[0]
    d_ff = wg.shape[1]
    return pl.pallas_call(
        functools.partial(_ffn_kernel, final_norm=final_norm),
        grid=(t // tm, d_ff // tf),
        in_specs=[
            pl.BlockSpec((tm, D_MODEL), lambda i, j: (i, 0)),
            pl.BlockSpec((1, D_MODEL), lambda i, j: (0, 0)),
            pl.BlockSpec((D_MODEL, tf), lambda i, j: (0, j)),
            pl.BlockSpec((D_MODEL, tf), lambda i, j: (0, j)),
            pl.BlockSpec((tf, D_MODEL), lambda i, j: (j, 0)),
            pl.BlockSpec((1, D_MODEL), lambda i, j: (0, 0)),
        ],
        out_specs=pl.BlockSpec((tm, D_MODEL), lambda i, j: (i, 0)),
        out_shape=jax.ShapeDtypeStruct((t, D_MODEL), F32),
        scratch_shapes=[pltpu.VMEM((tm, D_MODEL), BF16)],
        compiler_params=pltpu.CompilerParams(
            dimension_semantics=("parallel", "arbitrary"), vmem_limit_bytes=VMEM_LIMIT),
        name="ffn",
    )(x2, gain, wg, wu, wd, gain_final)


def kernel(x, norm_mix, w_in, attn_sinks, ssm_lam_re, ssm_lam_im, ssm_log_dt, ssm_b_re, ssm_b_im,
           ssm_c_re, ssm_c_im, ssm_d, ssm_w_glu, gmlp_ln_g, gmlp_ln_b, gmlp_w_s, gmlp_b_s,
           out_norm_attn, out_norm_ssm, out_norm_gmlp, w_out, norm_ffn, w_gate, w_up, w_down,
           norm_final):
    bsz, seq, _ = x.shape
    depth = w_in.shape[0]
    t = bsz * seq
    nb = seq // ATTN_BLOCK
    chunks_per_seq = seq // SSM_CHUNK
    assert seq % 512 == 0 and (1 << SSM_SCAN_STEPS) == chunks_per_seq
    rows = t // SSM_CHUNK

    c0 = ATTN_WIDTH
    c1 = c0 + 2 * KV_WIDTH
    c2 = c1 + SSM_WIDTH
    c3 = c2 + GMLP_WIDTH

    causal = jnp.tril(jnp.ones((GMLP_CHUNK, GMLP_CHUNK), dtype=bool))
    head_of = ((jnp.arange(N_KV_HEADS)[None, :, None] * 4 + (jnp.arange(4 * ATTN_BLOCK) // ATTN_BLOCK)[:, None, None]) * 2
               + jnp.arange(2)[None, None, :]).reshape(4 * ATTN_BLOCK, 2 * N_KV_HEADS)

    x2 = x.reshape(t, D_MODEL)
    for l in range(depth):
        wl = w_in[l]
        wq = wl[:, :c0].astype(BF16)
        wkv = wl[:, c0:c1].astype(BF16)
        wst = wl[:, c1:c2].T.astype(BF16)
        wzu = wl[:, c2:c3].astype(BF16)
        wzv = wl[:, c3:].astype(BF16)
        q, kv, zu, zv, ut = _in_proj(x2, norm_mix[l][None], wq, wkv, wzu, wzv, wst, tm=512)

        sink_cols = attn_sinks[l].astype(F32)[head_of]
        y_attn = _attention(q, kv, sink_cols, out_norm_attn[l][None], bsz, nb)

        kc, e, f, ap = _ssm_tables(ssm_lam_re[l], ssm_lam_im[l], ssm_log_dt[l], ssm_b_re[l], ssm_b_im[l],
                                   ssm_c_re[l], ssm_c_im[l], ssm_d[l])
        u4 = ut.reshape(SSM_GROUPS, SSM_GROUP, rows, SSM_CHUNK)
        y4 = _ssm(u4, kc, e, f, ap, chunks_per_seq)
        yt = y4.reshape(SSM_WIDTH, t)
        y_ssm = _ssm_glu(yt, ssm_w_glu[l].T.astype(BF16), out_norm_ssm[l][:, None], tm=512)

        ws = jnp.where(causal[None], gmlp_w_s[l], 0.0).astype(BF16)
        y_gmlp = _gmlp(zu, zv, gmlp_ln_g[l][None], gmlp_ln_b[l][None], ws, gmlp_b_s[l].T,
                       out_norm_gmlp[l][None], tm=512)

        x2 = _out_proj(x2, y_attn, y_ssm, y_gmlp, w_out[l].astype(BF16), tm=512)
        x2 = _ffn(x2, norm_ffn[l][None], w_gate[l].astype(BF16), w_up[l].astype(BF16),
                  w_down[l].astype(BF16), norm_final[None], tm=512, tf=512,
                  final_norm=(l == depth - 1))
    return x2.reshape(bsz, seq, D_MODEL)
```

```python
import functools
import math

import jax
import jax.numpy as jnp
from jax import lax
from jax.experimental import pallas as pl
from jax.experimental.pallas import tpu as pltpu

F32 = jnp.float32
BF16 = jnp.bfloat16

D_MODEL = 2048
HEAD_DIM = 64
ATTN_WIDTH = 1024
N_Q_HEADS = ATTN_WIDTH // HEAD_DIM
N_KV_HEADS = 2
KV_WIDTH = N_KV_HEADS * HEAD_DIM
ATTN_BLOCK = 128
SSM_WIDTH = 512
SSM_GROUP = 16
SSM_GROUPS = SSM_WIDTH // SSM_GROUP
SSM_STATE = 64
GMLP_WIDTH = 512
GMLP_CHUNK = 128
GMLP_GROUPS = GMLP_WIDTH // 128
EPS = 1e-5
COL_KV = ATTN_WIDTH
COL_SSM = COL_KV + 2 * KV_WIDTH
COL_ZU = COL_SSM + SSM_WIDTH
COL_ZV = COL_ZU + GMLP_WIDTH
COL_END = COL_ZV + GMLP_WIDTH

LANES = 128
SSM_CHUNK = LANES
SSM_SCAN_STEPS = 7
VMEM_LIMIT = 56 * 1024 * 1024


def _gelu_tanh(x):
    c = math.sqrt(2.0 / math.pi)
    return 0.5 * x * (1.0 + jnp.tanh(c * (x + 0.044715 * (x * x * x))))


def _rms_lanes(y, gain):
    ms = jnp.mean(y * y, axis=-1, keepdims=True)
    return y * lax.rsqrt(ms + EPS) * gain


def _in_proj_kernel(x_ref, g_ref, w_ref, wst_ref, q_ref, kv_ref, zu_ref, zv_ref, ut_ref):
    h = _rms_lanes(x_ref[...], g_ref[...]).astype(BF16)

    def proj(lo, hi):
        return jnp.dot(h, w_ref[:, lo:hi], preferred_element_type=F32)

    q_ref[...] = (proj(0, COL_KV) * (HEAD_DIM ** -0.5)).astype(BF16)
    kv_ref[...] = proj(COL_KV, COL_SSM).astype(BF16)
    zu_ref[...] = proj(COL_ZU, COL_ZV).astype(BF16)
    zv_ref[...] = proj(COL_ZV, COL_END).astype(BF16)
    ut = lax.dot_general(wst_ref[...], h, (((1,), (1,)), ((), ())), preferred_element_type=F32)
    ut_ref[...] = ut.astype(BF16)


def _in_proj(x2, gain, w_in_bf, wst, layer, tm):
    t = x2.shape[0]
    const = lambda i: (0, 0)
    row = lambda i: (i, 0)
    return pl.pallas_call(
        _in_proj_kernel,
        grid=(t // tm,),
        in_specs=[
            pl.BlockSpec((tm, D_MODEL), row),
            pl.BlockSpec((1, D_MODEL), const),
            pl.BlockSpec((None,) + w_in_bf.shape[1:], lambda i: (layer, 0, 0)),
            pl.BlockSpec(wst.shape, const),
        ],
        out_specs=[
            pl.BlockSpec((tm, ATTN_WIDTH), row),
            pl.BlockSpec((tm, 2 * KV_WIDTH), row),
            pl.BlockSpec((tm, GMLP_WIDTH), row),
            pl.BlockSpec((tm, GMLP_WIDTH), row),
            pl.BlockSpec((SSM_WIDTH, tm), lambda i: (0, i)),
        ],
        out_shape=[
            jax.ShapeDtypeStruct((t, ATTN_WIDTH), BF16),
            jax.ShapeDtypeStruct((t, 2 * KV_WIDTH), BF16),
            jax.ShapeDtypeStruct((t, GMLP_WIDTH), BF16),
            jax.ShapeDtypeStruct((t, GMLP_WIDTH), BF16),
            jax.ShapeDtypeStruct((SSM_WIDTH, t), BF16),
        ],
        compiler_params=pltpu.CompilerParams(
            dimension_semantics=("parallel",), vmem_limit_bytes=VMEM_LIMIT),
        name="in_proj",
    )(x2, gain, w_in_bf, wst)


def _attn_kernel(q_ref, kvp_ref, kvc_ref, sink_ref, g_ref, o_ref, *, nblk):
    first_tile = pl.program_id(1) == 0
    lo = lax.broadcasted_iota(jnp.int32, (ATTN_BLOCK, LANES), 1) < HEAD_DIM

    def padded(kvblk):
        kvf = kvblk.astype(F32)
        kk, vv = kvf[:, :LANES], kvf[:, LANES:]
        kk_sw = pltpu.roll(kk, HEAD_DIM, axis=1)
        vv_sw = pltpu.roll(vv, HEAD_DIM, axis=1)
        ke = [jnp.where(lo, kk, 0.0), jnp.where(lo, kk_sw, 0.0)]
        ko = [jnp.where(lo, 0.0, kk_sw), jnp.where(lo, 0.0, kk)]
        ve = [jnp.where(lo, vv, 0.0), jnp.where(lo, vv_sw, 0.0)]
        vo = [jnp.where(lo, 0.0, vv_sw), jnp.where(lo, 0.0, vv)]
        return [[a.astype(BF16) for a in lst] for lst in (ke, ko, ve, vo)]

    keys = [padded(kvp_ref[...])] + [
        padded(kvc_ref[i * ATTN_BLOCK:(i + 1) * ATTN_BLOCK, :]) for i in range(nblk)]

    rows = 4 * ATTN_BLOCK
    qi = lax.broadcasted_iota(jnp.int32, (rows, 2 * ATTN_BLOCK), 0) & (ATTN_BLOCK - 1)
    ks = lax.broadcasted_iota(jnp.int32, (rows, 2 * ATTN_BLOCK), 1)
    band = (ks > qi) & (ks <= qi + ATTN_BLOCK)
    band_first = band & (jnp.logical_not(first_tile) | (ks >= ATTN_BLOCK))
    lo_rows = lax.broadcasted_iota(jnp.int32, (rows, LANES), 1) < HEAD_DIM
    even_key_rows = lax.broadcasted_iota(jnp.int32, (rows, LANES), 0) < 2 * ATTN_BLOCK
    ones = jnp.where(lo_rows == even_key_rows, 1.0, 0.0).astype(BF16)
    sinks = sink_ref[...]

    for i in range(nblk):
        q = q_ref[i * ATTN_BLOCK:(i + 1) * ATTN_BLOCK, :]
        mask = band_first if i == 0 else band
        outs = []
        for kh in range(N_KV_HEADS):
            kz = jnp.concatenate([keys[i][0][kh], keys[i + 1][0][kh],
                                  keys[i][1][kh], keys[i + 1][1][kh]], axis=0)
            vz = jnp.concatenate([keys[i][2][kh], keys[i + 1][2][kh],
                                  keys[i][3][kh], keys[i + 1][3][kh]], axis=0)
            qs = jnp.concatenate(
                [q[:, (kh * 4 + p) * LANES:(kh * 4 + p + 1) * LANES] for p in range(4)], axis=0)
            s = lax.dot_general(qs, kz, (((1,), (1,)), ((), ())), preferred_element_type=F32)
            es, ts = [], []
            for par in range(2):
                sp = jnp.where(mask, s[:, par * 256:(par + 1) * 256], -jnp.inf)
                sk = sinks[:, kh * 2 + par:kh * 2 + par + 1]
                m = jnp.maximum(jnp.max(sp, axis=-1, keepdims=True), sk)
                es.append(jnp.exp(sp - m).astype(BF16))
                ts.append(jnp.exp(sk - m))
            e = jnp.concatenate(es, axis=1)
            od = jnp.dot(e, jnp.concatenate([vz, ones], axis=1), preferred_element_type=F32)
            den = od[:, LANES:] + jnp.where(lo_rows, ts[0], ts[1])
            o = od[:, :LANES] / den
            outs.extend([o[pp * ATTN_BLOCK:(pp + 1) * ATTN_BLOCK] for pp in range(4)])
        y = jnp.concatenate(outs, axis=1)
        o_ref[i * ATTN_BLOCK:(i + 1) * ATTN_BLOCK, :] = _rms_lanes(y, g_ref[...]).astype(BF16)


def _attention(q, kv, sink_cols, gain, bsz, seq, nblk):
    t = q.shape[0]
    tq = nblk * ATTN_BLOCK
    nt = seq // tq
    return pl.pallas_call(
        functools.partial(_attn_kernel, nblk=nblk),
        grid=(bsz, nt),
        in_specs=[
            pl.BlockSpec((tq, ATTN_WIDTH), lambda b, n: (b * nt + n, 0)),
            pl.BlockSpec((ATTN_BLOCK, 2 * KV_WIDTH),
                         lambda b, n: ((b * nt + n) * nblk - jnp.minimum(n, 1), 0)),
            pl.BlockSpec((tq, 2 * KV_WIDTH), lambda b, n: (b * nt + n, 0)),
            pl.BlockSpec(sink_cols.shape, lambda b, n: (0, 0)),
            pl.BlockSpec((1, ATTN_WIDTH), lambda b, n: (0, 0)),
        ],
        out_specs=pl.BlockSpec((tq, ATTN_WIDTH), lambda b, n: (b * nt + n, 0)),
        out_shape=jax.ShapeDtypeStruct((t, ATTN_WIDTH), BF16),
        compiler_params=pltpu.CompilerParams(
            dimension_semantics=("parallel", "parallel"), vmem_limit_bytes=VMEM_LIMIT),
        name="swa_attention",
    )(q, kv, kv, sink_cols, gain)


def _ssm_kernel(u_ref, kc_ref, e_ref, f_ref, ap_ref, y_ref, m_scr, *, chunks_per_seq):
    rows = u_ref.shape[2]
    tri = (lax.broadcasted_iota(jnp.int32, (LANES, LANES), 1)
           >= lax.broadcasted_iota(jnp.int32, (LANES, LANES), 0))

    def build(hp, carry):
        for h in range(SSM_GROUP):
            k = kc_ref[0, pl.ds(hp * SSM_GROUP + h, 1), :]
            kb = jnp.broadcast_to(k, (LANES, LANES))
            kr = pltpu.roll(kb, 0, 1, stride=1, stride_axis=0)
            blk = jnp.where(tri, kr, 0.0).astype(BF16)
            m_scr[pl.ds(pl.multiple_of(hp * LANES, LANES), LANES), h * LANES:(h + 1) * LANES] = blk
        return carry

    lax.fori_loop(0, SSM_GROUP, build, 0)

    u = jnp.concatenate([u_ref[0, hp] for hp in range(SSM_GROUP)], axis=1)
    y = jnp.dot(u, m_scr[...], preferred_element_type=F32)
    s = jnp.dot(u, e_ref[0], preferred_element_type=F32)

    cidx = lax.broadcasted_iota(jnp.int32, (rows, LANES), 0) & (chunks_per_seq - 1)
    for j in range(SSM_SCAN_STEPS):
        d = 1 << j
        prev = jnp.where(cidx >= d, pltpu.roll(s, d, axis=0), 0.0)
        a_same = ap_ref[0, 2 * j:2 * j + 1, :]
        a_cross = ap_ref[0, 2 * j + 1:2 * j + 2, :]
        s = s + prev * a_same + pltpu.roll(prev, SSM_STATE, axis=1) * a_cross
    s_prev = jnp.where(cidx >= 1, pltpu.roll(s, 1, axis=0), 0.0)
    y = y + jnp.dot(s_prev.astype(BF16), f_ref[0], preferred_element_type=F32)
    for h in range(SSM_GROUP):
        y_ref[0, h] = y[:, h * LANES:(h + 1) * LANES].astype(BF16)


def _ssm(u4, kc, e, f, ap, chunks_per_seq):
    g, hh, rows, _ = u4.shape
    return pl.pallas_call(
        functools.partial(_ssm_kernel, chunks_per_seq=chunks_per_seq),
        grid=(g,),
        in_specs=[
            pl.BlockSpec((1, hh, rows, LANES), lambda i: (i, 0, 0, 0)),
            pl.BlockSpec((1,) + kc.shape[1:], lambda i: (i, 0, 0)),
            pl.BlockSpec((1,) + e.shape[1:], lambda i: (i, 0, 0)),
            pl.BlockSpec((1,) + f.shape[1:], lambda i: (i, 0, 0)),
            pl.BlockSpec((1,) + ap.shape[1:], lambda i: (i, 0, 0)),
        ],
        out_specs=pl.BlockSpec((1, hh, rows, LANES), lambda i: (i, 0, 0, 0)),
        out_shape=jax.ShapeDtypeStruct(u4.shape, BF16),
        scratch_shapes=[pltpu.VMEM((SSM_GROUP * LANES, SSM_GROUP * LANES), BF16)],
        compiler_params=pltpu.CompilerParams(
            dimension_semantics=("parallel",), vmem_limit_bytes=VMEM_LIMIT),
        name="s5_chunked",
    )(u4, kc, e, f, ap)


def _ssm_tables(lam_re, lam_im, log_dt, b_re, b_im, c_re, c_im, d_skip):
    hp = lax.Precision.HIGHEST
    lr, li = lam_re.astype(F32), lam_im.astype(F32)
    dt = jnp.exp(log_dt.astype(F32))[:, None]
    mag = jnp.exp(lr * dt)
    ar, ai = mag * jnp.cos(li * dt), mag * jnp.sin(li * dt)
    den = lr * lr + li * li
    nre, nim = ar - 1.0, ai
    coef_re = (nre * lr + nim * li) / den
    coef_im = (nim * lr - nre * li) / den
    br, bi = b_re.astype(F32), b_im.astype(F32)
    bbr = coef_re[..., None] * br - coef_im[..., None] * bi
    bbi = coef_re[..., None] * bi + coef_im[..., None] * br
    cr, ci = c_re.astype(F32), c_im.astype(F32)
    n = jnp.arange(SSM_CHUNK + 1, dtype=F32)[:, None, None]
    pmag = jnp.exp(n * (lr * dt)[None])
    pr, pi = pmag * jnp.cos(n * (li * dt)[None]), pmag * jnp.sin(n * (li * dt)[None])
    crp = jnp.swapaxes(cr, 1, 2)
    cip = jnp.swapaxes(ci, 1, 2)
    wr = bbr[:, :, :, None] * crp[:, :, None, :] - bbi[:, :, :, None] * cip[:, :, None, :]
    wi = bbr[:, :, :, None] * cip[:, :, None, :] + bbi[:, :, :, None] * crp[:, :, None, :]
    taps = (jnp.einsum("ngp,gpab->gabn", pr[:SSM_CHUNK], wr, precision=hp)
            - jnp.einsum("ngp,gpab->gabn", pi[:SSM_CHUNK], wi, precision=hp))
    dskip = d_skip.astype(F32).reshape(SSM_GROUPS, SSM_GROUP)
    eye = jnp.eye(SSM_GROUP, dtype=F32)
    taps = taps.at[:, :, :, 0].add(eye[None] * dskip[:, None, :])
    kc = taps.reshape(SSM_GROUPS, SSM_GROUP * SSM_GROUP, SSM_CHUNK)
    prr = pr[:SSM_CHUNK][::-1]
    pir = pi[:SSM_CHUNK][::-1]
    bbr_t = jnp.swapaxes(bbr, 1, 2)
    bbi_t = jnp.swapaxes(bbi, 1, 2)
    prr_g = jnp.swapaxes(prr, 0, 1)[:, None]
    pir_g = jnp.swapaxes(pir, 0, 1)[:, None]
    e_re = prr_g * bbr_t[:, :, None, :] - pir_g * bbi_t[:, :, None, :]
    e_im = prr_g * bbi_t[:, :, None, :] + pir_g * bbr_t[:, :, None, :]
    e = jnp.concatenate([e_re, e_im], axis=-1).reshape(
        SSM_GROUPS, SSM_GROUP * SSM_CHUNK, 2 * SSM_STATE).astype(BF16)
    p1r = jnp.transpose(pr[1:], (1, 2, 0))[:, :, None, :]
    p1i = jnp.transpose(pi[1:], (1, 2, 0))[:, :, None, :]
    f_re = crp[:, :, :, None] * p1r - cip[:, :, :, None] * p1i
    f_im = crp[:, :, :, None] * p1i + cip[:, :, :, None] * p1r
    f = jnp.concatenate([f_re, -f_im], axis=1).reshape(
        SSM_GROUPS, 2 * SSM_STATE, SSM_GROUP * SSM_CHUNK).astype(BF16)
    a_r, a_i = pr[SSM_CHUNK], pi[SSM_CHUNK]
    rows = []
    for _ in range(SSM_SCAN_STEPS):
        rows.append(jnp.concatenate([a_r, a_r], axis=-1))
        rows.append(jnp.concatenate([-a_i, a_i], axis=-1))
        a_r, a_i = a_r * a_r - a_i * a_i, 2.0 * a_r * a_i
    rows.extend([jnp.zeros_like(rows[0])] * (16 - len(rows)))
    ap = jnp.stack(rows, axis=1)
    return kc, e, f, ap


def _glu_kernel(yt_ref, wt_ref, g_ref, o_ref):
    y = _gelu_tanh(yt_ref[...].astype(F32))
    z = jnp.dot(wt_ref[...], y.astype(BF16), preferred_element_type=F32)
    o = y * jax.nn.sigmoid(z)
    ms = jnp.mean(o * o, axis=0, keepdims=True)
    o = o * lax.rsqrt(ms + EPS) * g_ref[...]
    o_ref[...] = o.T.astype(BF16)


def _ssm_glu(yt, w_glu_t, gain_col, tm):
    t = yt.shape[1]
    return pl.pallas_call(
        _glu_kernel,
        grid=(t // tm,),
        in_specs=[
            pl.BlockSpec((SSM_WIDTH, tm), lambda i: (0, i)),
            pl.BlockSpec((SSM_WIDTH, SSM_WIDTH), lambda i: (0, 0)),
            pl.BlockSpec((SSM_WIDTH, 1), lambda i: (0, 0)),
        ],
        out_specs=pl.BlockSpec((tm, SSM_WIDTH), lambda i: (i, 0)),
        out_shape=jax.ShapeDtypeStruct((t, SSM_WIDTH), BF16),
        compiler_params=pltpu.CompilerParams(
            dimension_semantics=("parallel",), vmem_limit_bytes=VMEM_LIMIT),
        name="s5_glu",
    )(yt, w_glu_t, gain_col)


def _gmlp_kernel(zu_ref, zv_ref, lng_ref, lnb_ref, ws_ref, bs_ref, g_ref, o_ref):
    tm = zu_ref.shape[0]
    u = _gelu_tanh(zu_ref[...].astype(F32))
    v = _gelu_tanh(zv_ref[...].astype(F32))
    vc = v - jnp.mean(v, axis=-1, keepdims=True)
    v = vc * lax.rsqrt(jnp.mean(vc * vc, axis=-1, keepdims=True) + EPS) * lng_ref[...] + lnb_ref[...]
    vb = v.astype(BF16)
    bs = bs_ref[...]
    cols = []
    for g in range(GMLP_GROUPS):
        w = ws_ref[g]
        parts = []
        for c in range(tm // GMLP_CHUNK):
            vblk = vb[c * GMLP_CHUNK:(c + 1) * GMLP_CHUNK, g * LANES:(g + 1) * LANES]
            parts.append(jnp.dot(w, vblk, preferred_element_type=F32) + bs[:, g:g + 1])
        cols.append(jnp.concatenate(parts, axis=0))
    mixed = jnp.concatenate(cols, axis=1)
    o_ref[...] = _rms_lanes(u * mixed, g_ref[...]).astype(BF16)


def _gmlp(zu, zv, ln_g, ln_b, ws, bs_t, gain, tm):
    t = zu.shape[0]
    row = lambda i: (i, 0)
    const2 = lambda i: (0, 0)
    return pl.pallas_call(
        _gmlp_kernel,
        grid=(t // tm,),
        in_specs=[
            pl.BlockSpec((tm, GMLP_WIDTH), row),
            pl.BlockSpec((tm, GMLP_WIDTH), row),
            pl.BlockSpec((1, GMLP_WIDTH), const2),
            pl.BlockSpec((1, GMLP_WIDTH), const2),
            pl.BlockSpec(ws.shape, lambda i: (0, 0, 0)),
            pl.BlockSpec(bs_t.shape, const2),
            pl.BlockSpec((1, GMLP_WIDTH), const2),
        ],
        out_specs=pl.BlockSpec((tm, GMLP_WIDTH), row),
        out_shape=jax.ShapeDtypeStruct((t, GMLP_WIDTH), BF16),
        compiler_params=pltpu.CompilerParams(
            dimension_semantics=("parallel",), vmem_limit_bytes=VMEM_LIMIT),
        name="gmlp",
    )(zu, zv, ln_g, ln_b, ws, bs_t, gain)


def _out_proj_kernel(x_ref, ya_ref, ys_ref, yg_ref, w_ref, g_ref, o_ref, h_ref):
    y = jnp.concatenate([ya_ref[...], ys_ref[...], yg_ref[...]], axis=1)
    xn = x_ref[...] + jnp.dot(y, w_ref[...], preferred_element_type=F32)
    o_ref[...] = xn
    h_ref[...] = _rms_lanes(xn, g_ref[...]).astype(BF16)


def _out_proj(x2, ya, ys, yg, w_out_bf, gain_ffn, layer, tm):
    t = x2.shape[0]
    row = lambda i: (i, 0)
    return pl.pallas_call(
        _out_proj_kernel,
        grid=(t // tm,),
        in_specs=[
            pl.BlockSpec((tm, D_MODEL), row),
            pl.BlockSpec((tm, ATTN_WIDTH), row),
            pl.BlockSpec((tm, SSM_WIDTH), row),
            pl.BlockSpec((tm, GMLP_WIDTH), row),
            pl.BlockSpec((None,) + w_out_bf.shape[1:], lambda i: (layer, 0, 0)),
            pl.BlockSpec((1, D_MODEL), lambda i: (0, 0)),
        ],
        out_specs=[pl.BlockSpec((tm, D_MODEL), row), pl.BlockSpec((tm, D_MODEL), row)],
        out_shape=[jax.ShapeDtypeStruct((t, D_MODEL), F32), jax.ShapeDtypeStruct((t, D_MODEL), BF16)],
        compiler_params=pltpu.CompilerParams(
            dimension_semantics=("parallel",), vmem_limit_bytes=VMEM_LIMIT),
        name="out_proj",
    )(x2, ya, ys, yg, w_out_bf, gain_ffn)


FFN_RES_CHUNK = 256
FFN_RES_STEPS = D_MODEL // FFN_RES_CHUNK


def _ffn_kernel(h_ref, xc_ref, wg_ref, wu_ref, wd_ref, gf_ref, o_ref, *, final_norm, halves):
    j = pl.program_id(1)

    @pl.when(j == 0)
    def _():
        o_ref[...] = jnp.zeros_like(o_ref)

    @pl.when(j < FFN_RES_STEPS)
    def _():
        cols = pl.ds(pl.multiple_of(j * FFN_RES_CHUNK, FFN_RES_CHUNK), FFN_RES_CHUNK)
        o_ref[:, cols] += xc_ref[...]

    h = h_ref[...]
    hw = wg_ref.shape[1] // halves
    acc = None
    for c in range(halves):
        gate = jnp.dot(h, wg_ref[:, c * hw:(c + 1) * hw], preferred_element_type=F32)
        up = jnp.dot(h, wu_ref[:, c * hw:(c + 1) * hw], preferred_element_type=F32)
        a = (gate * jax.nn.sigmoid(gate) * up).astype(BF16)
        d = jnp.dot(a, wd_ref[c * hw:(c + 1) * hw, :], preferred_element_type=F32)
        acc = d if acc is None else acc + d
    o_ref[...] += acc

    if final_norm:
        @pl.when(j == pl.num_programs(1) - 1)
        def _():
            o_ref[...] = _rms_lanes(o_ref[...], gf_ref[...])


def _ffn(h, x2, wg_bf, wu_bf, wd_bf, gain_final, layer, tm, tf, final_norm):
    t = x2.shape[0]
    d_ff = wg_bf.shape[2]
    assert d_ff // tf >= FFN_RES_STEPS
    return pl.pallas_call(
        functools.partial(_ffn_kernel, final_norm=final_norm, halves=2),
        grid=(t // tm, d_ff // tf),
        in_specs=[
            pl.BlockSpec((tm, D_MODEL), lambda i, j: (i, 0)),
            pl.BlockSpec((tm, FFN_RES_CHUNK), lambda i, j: (i, jnp.minimum(j, FFN_RES_STEPS - 1))),
            pl.BlockSpec((None, D_MODEL, tf), lambda i, j: (layer, 0, j)),
            pl.BlockSpec((None, D_MODEL, tf), lambda i, j: (layer, 0, j)),
            pl.BlockSpec((None, tf, D_MODEL), lambda i, j: (layer, j, 0)),
            pl.BlockSpec((1, D_MODEL), lambda i, j: (0, 0)),
        ],
        out_specs=pl.BlockSpec((tm, D_MODEL), lambda i, j: (i, 0)),
        out_shape=jax.ShapeDtypeStruct((t, D_MODEL), F32),
        compiler_params=pltpu.CompilerParams(
            dimension_semantics=("parallel", "arbitrary"), vmem_limit_bytes=VMEM_LIMIT),
        name="ffn",
    )(h, x2, wg_bf, wu_bf, wd_bf, gain_final)


def kernel(x, norm_mix, w_in, attn_sinks, ssm_lam_re, ssm_lam_im, ssm_log_dt, ssm_b_re, ssm_b_im,
           ssm_c_re, ssm_c_im, ssm_d, ssm_w_glu, gmlp_ln_g, gmlp_ln_b, gmlp_w_s, gmlp_b_s,
           out_norm_attn, out_norm_ssm, out_norm_gmlp, w_out, norm_ffn, w_gate, w_up, w_down,
           norm_final):
    bsz, seq, _ = x.shape
    depth = w_in.shape[0]
    t = bsz * seq
    chunks_per_seq = seq // SSM_CHUNK
    assert seq % 1024 == 0 and (1 << SSM_SCAN_STEPS) == chunks_per_seq
    rows = t // SSM_CHUNK

    w_in_bf = w_in.astype(BF16)
    w_out_bf = w_out.astype(BF16)
    w_gate_bf = w_gate.astype(BF16)
    w_up_bf = w_up.astype(BF16)
    w_down_bf = w_down.astype(BF16)

    causal = jnp.tril(jnp.ones((GMLP_CHUNK, GMLP_CHUNK), dtype=bool))
    head_of = ((jnp.arange(N_KV_HEADS)[None, :, None] * 4 + (jnp.arange(4 * ATTN_BLOCK) // ATTN_BLOCK)[:, None, None]) * 2
               + jnp.arange(2)[None, None, :]).reshape(4 * ATTN_BLOCK, 2 * N_KV_HEADS)

    x2 = x.reshape(t, D_MODEL)
    for l in range(depth):
        wst = w_in[l][:, COL_SSM:COL_ZU].T.astype(BF16)
        q, kv, zu, zv, ut = _in_proj(x2, norm_mix[l][None], w_in_bf, wst, l, tm=512)

        sink_cols = attn_sinks[l].astype(F32)[head_of]
        y_attn = _attention(q, kv, sink_cols, out_norm_attn[l][None], bsz, seq, nblk=4)

        kc, e, f, ap = _ssm_tables(ssm_lam_re[l], ssm_lam_im[l], ssm_log_dt[l], ssm_b_re[l], ssm_b_im[l],
                                   ssm_c_re[l], ssm_c_im[l], ssm_d[l])
        u4 = ut.reshape(SSM_GROUPS, SSM_GROUP, rows, SSM_CHUNK)
        y4 = _ssm(u4, kc, e, f, ap, chunks_per_seq)
        yt = y4.reshape(SSM_WIDTH, t)
        y_ssm = _ssm_glu(yt, ssm_w_glu[l].T.astype(BF16), out_norm_ssm[l][:, None], tm=512)

        ws = jnp.where(causal[None], gmlp_w_s[l], 0.0).astype(BF16)
        y_gmlp = _gmlp(zu, zv, gmlp_ln_g[l][None], gmlp_ln_b[l][None], ws, gmlp_b_s[l].T,
                       out_norm_gmlp[l][None], tm=512)

        x2, h = _out_proj(x2, y_attn, y_ssm, y_gmlp, w_out_bf, norm_ffn[l][None], l, tm=512)
        x2 = _ffn(h, x2, w_gate_bf, w_up_bf, w_down_bf, norm_final[None], l, tm=1024, tf=512,
                  final_norm=(l == depth - 1))
    return x2.reshape(bsz, seq, D_MODEL)
```

```python
import functools
import math

import jax
import jax.numpy as jnp
from jax import lax
from jax.experimental import pallas as pl
from jax.experimental.pallas import tpu as pltpu

F32 = jnp.float32
BF16 = jnp.bfloat16

D_MODEL = 2048
HEAD_DIM = 64
ATTN_WIDTH = 1024
N_Q_HEADS = ATTN_WIDTH // HEAD_DIM
N_KV_HEADS = 2
KV_WIDTH = N_KV_HEADS * HEAD_DIM
ATTN_BLOCK = 128
SSM_WIDTH = 512
SSM_GROUP = 16
SSM_GROUPS = SSM_WIDTH // SSM_GROUP
SSM_STATE = 64
GMLP_WIDTH = 512
GMLP_CHUNK = 128
GMLP_GROUPS = GMLP_WIDTH // 128
EPS = 1e-5
LOG2E = math.log2(math.e)
COL_KV = ATTN_WIDTH
COL_SSM = COL_KV + 2 * KV_WIDTH
COL_ZU = COL_SSM + SSM_WIDTH
COL_ZV = COL_ZU + GMLP_WIDTH
COL_END = COL_ZV + GMLP_WIDTH

LANES = 128
SSM_CHUNK = LANES
SSM_SCAN_STEPS = 7
VMEM_LIMIT = 56 * 1024 * 1024


def _gelu_tanh(x):
    c = math.sqrt(2.0 / math.pi)
    return 0.5 * x * (1.0 + jnp.tanh(c * (x + 0.044715 * (x * x * x))))


def _rms_lanes(y, gain):
    ms = jnp.mean(y * y, axis=-1, keepdims=True)
    return y * lax.rsqrt(ms + EPS) * gain


def _in_proj_kernel(x_ref, g_ref, w_ref, wst_ref, q_ref, kv_ref, zu_ref, zv_ref, ut_ref):
    h = _rms_lanes(x_ref[...], g_ref[...]).astype(BF16)

    def proj(lo, hi):
        return jnp.dot(h, w_ref[:, lo:hi], preferred_element_type=F32)

    q_ref[...] = (proj(0, COL_KV) * (HEAD_DIM ** -0.5 * LOG2E)).astype(BF16)
    kv_ref[...] = proj(COL_KV, COL_SSM).astype(BF16)
    zu_ref[...] = proj(COL_ZU, COL_ZV).astype(BF16)
    zv_ref[...] = proj(COL_ZV, COL_END).astype(BF16)
    ut = lax.dot_general(wst_ref[...], h, (((1,), (1,)), ((), ())), preferred_element_type=F32)
    ut_ref[...] = ut.astype(BF16)


def _in_proj(x2, gain, w_in_bf, wst, layer, tm):
    t = x2.shape[0]
    const = lambda i: (0, 0)
    row = lambda i: (i, 0)
    return pl.pallas_call(
        _in_proj_kernel,
        grid=(t // tm,),
        in_specs=[
            pl.BlockSpec((tm, D_MODEL), row),
            pl.BlockSpec((1, D_MODEL), const),
            pl.BlockSpec((None,) + w_in_bf.shape[1:], lambda i: (layer, 0, 0),
                         pipeline_mode=pl.Buffered(1)),
            pl.BlockSpec(wst.shape, const),
        ],
        out_specs=[
            pl.BlockSpec((tm, ATTN_WIDTH), row),
            pl.BlockSpec((tm, 2 * KV_WIDTH), row),
            pl.BlockSpec((tm, GMLP_WIDTH), row),
            pl.BlockSpec((tm, GMLP_WIDTH), row),
            pl.BlockSpec((SSM_WIDTH, tm), lambda i: (0, i)),
        ],
        out_shape=[
            jax.ShapeDtypeStruct((t, ATTN_WIDTH), BF16),
            jax.ShapeDtypeStruct((t, 2 * KV_WIDTH), BF16),
            jax.ShapeDtypeStruct((t, GMLP_WIDTH), BF16),
            jax.ShapeDtypeStruct((t, GMLP_WIDTH), BF16),
            jax.ShapeDtypeStruct((SSM_WIDTH, t), BF16),
        ],
        compiler_params=pltpu.CompilerParams(
            dimension_semantics=("parallel",), vmem_limit_bytes=VMEM_LIMIT),
        name="in_proj",
    )(x2, gain, w_in_bf, wst)


def _attn_kernel(q_ref, kvp_ref, kvc_ref, sink_ref, g_ref, o_ref, *, nblk):
    first_tile = pl.program_id(1) == 0
    lo = lax.broadcasted_iota(jnp.int32, (ATTN_BLOCK, LANES), 1) < HEAD_DIM

    def padded(kvblk):
        kvf = kvblk.astype(F32)
        kk, vv = kvf[:, :LANES], kvf[:, LANES:]
        kk_sw = pltpu.roll(kk, HEAD_DIM, axis=1)
        vv_sw = pltpu.roll(vv, HEAD_DIM, axis=1)
        ke = [jnp.where(lo, kk, 0.0), jnp.where(lo, kk_sw, 0.0)]
        ko = [jnp.where(lo, 0.0, kk_sw), jnp.where(lo, 0.0, kk)]
        ve = [jnp.where(lo, vv, 0.0), jnp.where(lo, vv_sw, 0.0)]
        vo = [jnp.where(lo, 0.0, vv_sw), jnp.where(lo, 0.0, vv)]
        return [[a.astype(BF16) for a in lst] for lst in (ke, ko, ve, vo)]

    keys = [padded(kvp_ref[...])] + [
        padded(kvc_ref[i * ATTN_BLOCK:(i + 1) * ATTN_BLOCK, :]) for i in range(nblk)]

    rows = 4 * ATTN_BLOCK
    qi = lax.broadcasted_iota(jnp.int32, (rows, LANES), 0) & (ATTN_BLOCK - 1)
    ci = lax.broadcasted_iota(jnp.int32, (rows, LANES), 1)
    vis_prev = ci > qi
    vis_prev_first = vis_prev & jnp.logical_not(first_tile)
    lo_rows = ci < HEAD_DIM
    even_key_rows = lax.broadcasted_iota(jnp.int32, (rows, LANES), 0) < 2 * ATTN_BLOCK
    ones = jnp.where(lo_rows == even_key_rows, 1.0, 0.0).astype(BF16)

    for i in range(nblk):
        q = q_ref[i * ATTN_BLOCK:(i + 1) * ATTN_BLOCK, :]
        vp = vis_prev_first if i == 0 else vis_prev
        outs = []
        for kh in range(N_KV_HEADS):
            kz = jnp.concatenate([keys[i][0][kh], keys[i + 1][0][kh],
                                  keys[i][1][kh], keys[i + 1][1][kh]], axis=0)
            vz = jnp.concatenate([keys[i][2][kh], keys[i + 1][2][kh],
                                  keys[i][3][kh], keys[i + 1][3][kh]], axis=0)
            qs = jnp.concatenate(
                [q[:, (kh * 4 + p) * LANES:(kh * 4 + p + 1) * LANES] for p in range(4)], axis=0)
            s = lax.dot_general(qs, kz, (((1,), (1,)), ((), ())), preferred_element_type=F32)
            es, ts = [], []
            for par in range(2):
                c0 = par * 2 * ATTN_BLOCK
                sp = jnp.where(vp, s[:, c0:c0 + ATTN_BLOCK], -jnp.inf)
                sc = jnp.where(vis_prev, -jnp.inf, s[:, c0 + ATTN_BLOCK:c0 + 2 * ATTN_BLOCK])
                sk = sink_ref[kh * 2 + par]
                m = jnp.maximum(jnp.max(jnp.maximum(sp, sc), axis=-1, keepdims=True), sk)
                es.append(jnp.exp2(sp - m).astype(BF16))
                es.append(jnp.exp2(sc - m).astype(BF16))
                ts.append(jnp.exp2(sk - m))
            e = jnp.concatenate(es, axis=1)
            od = jnp.dot(e, jnp.concatenate([vz, ones], axis=1), preferred_element_type=F32)
            den = od[:, LANES:] + jnp.where(lo_rows, ts[0], ts[1])
            o = od[:, :LANES] / den
            outs.extend([o[pp * ATTN_BLOCK:(pp + 1) * ATTN_BLOCK] for pp in range(4)])
        y = jnp.concatenate(outs, axis=1)
        o_ref[i * ATTN_BLOCK:(i + 1) * ATTN_BLOCK, :] = _rms_lanes(y, g_ref[...]).astype(BF16)


def _attention(q, kv, sink_cols, gain, bsz, seq, nblk):
    t = q.shape[0]
    tq = nblk * ATTN_BLOCK
    nt = seq // tq
    return pl.pallas_call(
        functools.partial(_attn_kernel, nblk=nblk),
        grid=(bsz, nt),
        in_specs=[
            pl.BlockSpec((tq, ATTN_WIDTH), lambda b, n: (b * nt + n, 0)),
            pl.BlockSpec((ATTN_BLOCK, 2 * KV_WIDTH),
                         lambda b, n: ((b * nt + n) * nblk - jnp.minimum(n, 1), 0)),
            pl.BlockSpec((tq, 2 * KV_WIDTH), lambda b, n: (b * nt + n, 0)),
            pl.BlockSpec(sink_cols.shape, lambda b, n: (0, 0, 0)),
            pl.BlockSpec((1, ATTN_WIDTH), lambda b, n: (0, 0)),
        ],
        out_specs=pl.BlockSpec((tq, ATTN_WIDTH), lambda b, n: (b * nt + n, 0)),
        out_shape=jax.ShapeDtypeStruct((t, ATTN_WIDTH), BF16),
        compiler_params=pltpu.CompilerParams(
            dimension_semantics=("parallel", "parallel"), vmem_limit_bytes=VMEM_LIMIT),
        name="swa_attention",
    )(q, kv, kv, sink_cols, gain)


def _ssm_kernel(u_ref, kfirst_ref, kodd_ref, knext_ref, e_ref, f_ref, ap_ref, y_ref, m_even, m_odd,
                *, chunks_per_seq):
    rows = u_ref.shape[2]
    tri = (lax.broadcasted_iota(jnp.int32, (LANES, LANES), 1)
           >= lax.broadcasted_iota(jnp.int32, (LANES, LANES), 0))

    def build(k_ref, m_ref):
        for hp in range(SSM_GROUP):
            for h in range(SSM_GROUP):
                k = k_ref[0, hp * SSM_GROUP + h:hp * SSM_GROUP + h + 1, :]
                kb = jnp.broadcast_to(k, (LANES, LANES))
                kr = pltpu.roll(kb, 0, 1, stride=1, stride_axis=0)
                m_ref[hp * LANES:(hp + 1) * LANES, h * LANES:(h + 1) * LANES] = (
                    jnp.where(tri, kr, 0.0).astype(BF16))

    cidx = lax.broadcasted_iota(jnp.int32, (rows, LANES), 0) & (chunks_per_seq - 1)

    def mix(slot, m_ref):
        u = jnp.concatenate([u_ref[slot, hp] for hp in range(SSM_GROUP)], axis=1)
        y = jnp.dot(u, m_ref[...], preferred_element_type=F32)
        s = jnp.dot(u, e_ref[slot], preferred_element_type=F32)
        for j in range(SSM_SCAN_STEPS):
            d = 1 << j
            prev = jnp.where(cidx >= d, pltpu.roll(s, d, axis=0), 0.0)
            a_same = ap_ref[slot, 2 * j:2 * j + 1, :]
            a_cross = ap_ref[slot, 2 * j + 1:2 * j + 2, :]
            s = s + prev * a_same + pltpu.roll(prev, SSM_STATE, axis=1) * a_cross
        s_prev = jnp.where(cidx >= 1, pltpu.roll(s, 1, axis=0), 0.0)
        y = y + jnp.dot(s_prev.astype(BF16), f_ref[slot], preferred_element_type=F32)
        for h in range(SSM_GROUP):
            y_ref[slot, h] = y[:, h * LANES:(h + 1) * LANES].astype(BF16)

    @pl.when(pl.program_id(0) == 0)
    def _():
        build(kfirst_ref, m_even)

    mix(0, m_even)
    build(kodd_ref, m_odd)
    mix(1, m_odd)
    build(knext_ref, m_even)


def _ssm(u4, kc, e, f, ap, layer, chunks_per_seq):
    g, hh, rows, _ = u4.shape
    kblk = (None, 1) + kc.shape[2:]
    return pl.pallas_call(
        functools.partial(_ssm_kernel, chunks_per_seq=chunks_per_seq),
        grid=(g // 2,),
        in_specs=[
            pl.BlockSpec((2, hh, rows, LANES), lambda i: (i, 0, 0, 0)),
            pl.BlockSpec(kblk, lambda i: (layer, 0, 0, 0)),
            pl.BlockSpec(kblk, lambda i: (layer, 2 * i + 1, 0, 0)),
            pl.BlockSpec(kblk, lambda i: (layer, jnp.minimum(2 * i + 2, g - 1), 0, 0)),
            pl.BlockSpec((None, 2) + e.shape[2:], lambda i: (layer, i, 0, 0)),
            pl.BlockSpec((None, 2) + f.shape[2:], lambda i: (layer, i, 0, 0)),
            pl.BlockSpec((None, 2) + ap.shape[2:], lambda i: (layer, i, 0, 0)),
        ],
        out_specs=pl.BlockSpec((2, hh, rows, LANES), lambda i: (i, 0, 0, 0)),
        out_shape=jax.ShapeDtypeStruct(u4.shape, BF16),
        scratch_shapes=[pltpu.VMEM((SSM_GROUP * LANES, SSM_GROUP * LANES), BF16),
                        pltpu.VMEM((SSM_GROUP * LANES, SSM_GROUP * LANES), BF16)],
        compiler_params=pltpu.CompilerParams(
            dimension_semantics=("arbitrary",), vmem_limit_bytes=VMEM_LIMIT),
        name="s5_chunked",
    )(u4, kc, kc, kc, e, f, ap)


def _ssm_tables(lam_re, lam_im, log_dt, b_re, b_im, c_re, c_im, d_skip):
    hp = lax.Precision.HIGHEST
    nl = lam_re.shape[0]
    lr, li = lam_re.astype(F32), lam_im.astype(F32)
    dt = jnp.exp(log_dt.astype(F32))[..., None]
    mag = jnp.exp(lr * dt)
    ar, ai = mag * jnp.cos(li * dt), mag * jnp.sin(li * dt)
    den = lr * lr + li * li
    nre, nim = ar - 1.0, ai
    coef_re = (nre * lr + nim * li) / den
    coef_im = (nim * lr - nre * li) / den
    br, bi = b_re.astype(F32), b_im.astype(F32)
    bbr = coef_re[..., None] * br - coef_im[..., None] * bi
    bbi = coef_re[..., None] * bi + coef_im[..., None] * br
    crp = jnp.swapaxes(c_re.astype(F32), -1, -2)
    cip = jnp.swapaxes(c_im.astype(F32), -1, -2)
    n = jnp.arange(SSM_CHUNK + 1, dtype=F32)
    pmag = jnp.exp((lr * dt)[..., None] * n)
    pang = (li * dt)[..., None] * n
    pr, pi = pmag * jnp.cos(pang), pmag * jnp.sin(pang)
    wr = bbr[..., :, None] * crp[..., None, :] - bbi[..., :, None] * cip[..., None, :]
    wi = bbr[..., :, None] * cip[..., None, :] + bbi[..., :, None] * crp[..., None, :]
    taps = (jnp.einsum("lgpn,lgpab->lgabn", pr[..., :SSM_CHUNK], wr, precision=hp)
            - jnp.einsum("lgpn,lgpab->lgabn", pi[..., :SSM_CHUNK], wi, precision=hp))
    dskip = d_skip.astype(F32).reshape(nl, SSM_GROUPS, SSM_GROUP)
    taps = taps.at[..., 0].add(jnp.eye(SSM_GROUP, dtype=F32) * dskip[:, :, None, :])
    kc = taps.reshape(nl, SSM_GROUPS, SSM_GROUP * SSM_GROUP, SSM_CHUNK)
    prr = jnp.swapaxes(pr[..., :SSM_CHUNK][..., ::-1], -1, -2)[:, :, None]
    pir = jnp.swapaxes(pi[..., :SSM_CHUNK][..., ::-1], -1, -2)[:, :, None]
    bbr_t = jnp.swapaxes(bbr, -1, -2)[:, :, :, None, :]
    bbi_t = jnp.swapaxes(bbi, -1, -2)[:, :, :, None, :]
    e = jnp.concatenate([prr * bbr_t - pir * bbi_t, prr * bbi_t + pir * bbr_t], axis=-1).reshape(
        nl, SSM_GROUPS, SSM_GROUP * SSM_CHUNK, 2 * SSM_STATE).astype(BF16)
    p1r = pr[..., None, 1:]
    p1i = pi[..., None, 1:]
    f_re = crp[..., None] * p1r - cip[..., None] * p1i
    f_im = crp[..., None] * p1i + cip[..., None] * p1r
    f = jnp.concatenate([f_re, -f_im], axis=2).reshape(
        nl, SSM_GROUPS, 2 * SSM_STATE, SSM_GROUP * SSM_CHUNK).astype(BF16)
    a_r, a_i = pr[..., SSM_CHUNK], pi[..., SSM_CHUNK]
    rows = []
    for _ in range(SSM_SCAN_STEPS):
        rows.append(jnp.concatenate([a_r, a_r], axis=-1))
        rows.append(jnp.concatenate([-a_i, a_i], axis=-1))
        a_r, a_i = a_r * a_r - a_i * a_i, 2.0 * a_r * a_i
    rows.extend([jnp.zeros_like(rows[0])] * (16 - len(rows)))
    ap = jnp.stack(rows, axis=2)
    return kc, e, f, ap


def _glu_kernel(yt_ref, wt_ref, g_ref, o_ref):
    y = _gelu_tanh(yt_ref[...].astype(F32))
    z = jnp.dot(wt_ref[...], y.astype(BF16), preferred_element_type=F32)
    o = y * jax.nn.sigmoid(z)
    ms = jnp.mean(o * o, axis=0, keepdims=True)
    o = o * lax.rsqrt(ms + EPS) * g_ref[...]
    o_ref[...] = o.T.astype(BF16)


def _ssm_glu(yt, w_glu_t, gain_col, tm):
    t = yt.shape[1]
    return pl.pallas_call(
        _glu_kernel,
        grid=(t // tm,),
        in_specs=[
            pl.BlockSpec((SSM_WIDTH, tm), lambda i: (0, i)),
            pl.BlockSpec((SSM_WIDTH, SSM_WIDTH), lambda i: (0, 0)),
            pl.BlockSpec((SSM_WIDTH, 1), lambda i: (0, 0)),
        ],
        out_specs=pl.BlockSpec((tm, SSM_WIDTH), lambda i: (i, 0)),
        out_shape=jax.ShapeDtypeStruct((t, SSM_WIDTH), BF16),
        compiler_params=pltpu.CompilerParams(
            dimension_semantics=("parallel",), vmem_limit_bytes=VMEM_LIMIT),
        name="s5_glu",
    )(yt, w_glu_t, gain_col)


def _gmlp_kernel(zu_ref, zv_ref, lng_ref, lnb_ref, ws_ref, bs_ref, g_ref, o_ref):
    tm = zu_ref.shape[0]
    u = _gelu_tanh(zu_ref[...].astype(F32))
    v = _gelu_tanh(zv_ref[...].astype(F32))
    vc = v - jnp.mean(v, axis=-1, keepdims=True)
    v = vc * lax.rsqrt(jnp.mean(vc * vc, axis=-1, keepdims=True) + EPS) * lng_ref[...] + lnb_ref[...]
    vb = v.astype(BF16)
    bs = bs_ref[...]
    cols = []
    for g in range(GMLP_GROUPS):
        w = ws_ref[g]
        parts = []
        for c in range(tm // GMLP_CHUNK):
            vblk = vb[c * GMLP_CHUNK:(c + 1) * GMLP_CHUNK, g * LANES:(g + 1) * LANES]
            parts.append(jnp.dot(w, vblk, preferred_element_type=F32) + bs[:, g:g + 1])
        cols.append(jnp.concatenate(parts, axis=0))
    mixed = jnp.concatenate(cols, axis=1)
    o_ref[...] = _rms_lanes(u * mixed, g_ref[...]).astype(BF16)


def _gmlp(zu, zv, ln_g, ln_b, ws, bs_t, gain, tm):
    t = zu.shape[0]
    row = lambda i: (i, 0)
    const2 = lambda i: (0, 0)
    return pl.pallas_call(
        _gmlp_kernel,
        grid=(t // tm,),
        in_specs=[
            pl.BlockSpec((tm, GMLP_WIDTH), row),
            pl.BlockSpec((tm, GMLP_WIDTH), row),
            pl.BlockSpec((1, GMLP_WIDTH), const2),
            pl.BlockSpec((1, GMLP_WIDTH), const2),
            pl.BlockSpec(ws.shape, lambda i: (0, 0, 0)),
            pl.BlockSpec(bs_t.shape, const2),
            pl.BlockSpec((1, GMLP_WIDTH), const2),
        ],
        out_specs=pl.BlockSpec((tm, GMLP_WIDTH), row),
        out_shape=jax.ShapeDtypeStruct((t, GMLP_WIDTH), BF16),
        compiler_params=pltpu.CompilerParams(
            dimension_semantics=("parallel",), vmem_limit_bytes=VMEM_LIMIT),
        name="gmlp",
    )(zu, zv, ln_g, ln_b, ws, bs_t, gain)


def _out_proj_kernel(x_ref, ya_ref, ys_ref, yg_ref, w_ref, g_ref, o_ref, h_ref):
    y = jnp.concatenate([ya_ref[...], ys_ref[...], yg_ref[...]], axis=1)
    xn = x_ref[...] + jnp.dot(y, w_ref[...], preferred_element_type=F32)
    o_ref[...] = xn
    h_ref[...] = _rms_lanes(xn, g_ref[...]).astype(BF16)


def _out_proj(x2, ya, ys, yg, w_out_bf, gain_ffn, layer, tm):
    t = x2.shape[0]
    row = lambda i: (i, 0)
    return pl.pallas_call(
        _out_proj_kernel,
        grid=(t // tm,),
        in_specs=[
            pl.BlockSpec((tm, D_MODEL), row),
            pl.BlockSpec((tm, ATTN_WIDTH), row),
            pl.BlockSpec((tm, SSM_WIDTH), row),
            pl.BlockSpec((tm, GMLP_WIDTH), row),
            pl.BlockSpec((None,) + w_out_bf.shape[1:], lambda i: (layer, 0, 0),
                         pipeline_mode=pl.Buffered(1)),
            pl.BlockSpec((1, D_MODEL), lambda i: (0, 0)),
        ],
        out_specs=[pl.BlockSpec((tm, D_MODEL), row), pl.BlockSpec((tm, D_MODEL), row)],
        out_shape=[jax.ShapeDtypeStruct((t, D_MODEL), F32), jax.ShapeDtypeStruct((t, D_MODEL), BF16)],
        compiler_params=pltpu.CompilerParams(
            dimension_semantics=("parallel",), vmem_limit_bytes=VMEM_LIMIT),
        name="out_proj",
    )(x2, ya, ys, yg, w_out_bf, gain_ffn)


FFN_RES_CHUNK = 256
FFN_RES_STEPS = D_MODEL // FFN_RES_CHUNK


def _ffn_kernel(h_ref, xc_ref, wg_ref, wu_ref, wd_ref, gf_ref, o_ref, *, final_norm, halves):
    j = pl.program_id(1)

    @pl.when(j == 0)
    def _():
        o_ref[...] = jnp.zeros_like(o_ref)

    @pl.when(j < FFN_RES_STEPS)
    def _():
        cols = pl.ds(pl.multiple_of(j * FFN_RES_CHUNK, FFN_RES_CHUNK), FFN_RES_CHUNK)
        o_ref[:, cols] += xc_ref[...]

    h = h_ref[...]
    hw = wg_ref.shape[1] // halves
    acc = None
    for c in range(halves):
        gate = jnp.dot(h, wg_ref[:, c * hw:(c + 1) * hw], preferred_element_type=F32)
        up = jnp.dot(h, wu_ref[:, c * hw:(c + 1) * hw], preferred_element_type=F32)
        a = (gate * jax.nn.sigmoid(gate) * up).astype(BF16)
        d = jnp.dot(a, wd_ref[c * hw:(c + 1) * hw, :], preferred_element_type=F32)
        acc = d if acc is None else acc + d
    o_ref[...] += acc

    if final_norm:
        @pl.when(j == pl.num_programs(1) - 1)
        def _():
            o_ref[...] = _rms_lanes(o_ref[...], gf_ref[...])


def _ffn(h, x2, wg_bf, wu_bf, wd_bf, gain_final, layer, tm, tf, final_norm):
    t = x2.shape[0]
    d_ff = wg_bf.shape[2]
    assert d_ff // tf >= FFN_RES_STEPS
    return pl.pallas_call(
        functools.partial(_ffn_kernel, final_norm=final_norm, halves=2),
        grid=(t // tm, d_ff // tf),
        in_specs=[
            pl.BlockSpec((tm, D_MODEL), lambda i, j: (i, 0)),
            pl.BlockSpec((tm, FFN_RES_CHUNK), lambda i, j: (i, jnp.minimum(j, FFN_RES_STEPS - 1))),
            pl.BlockSpec((None, D_MODEL, tf), lambda i, j: (layer, 0, j)),
            pl.BlockSpec((None, D_MODEL, tf), lambda i, j: (layer, 0, j)),
            pl.BlockSpec((None, tf, D_MODEL), lambda i, j: (layer, j, 0)),
            pl.BlockSpec((1, D_MODEL), lambda i, j: (0, 0)),
        ],
        out_specs=pl.BlockSpec((tm, D_MODEL), lambda i, j: (i, 0)),
        out_shape=jax.ShapeDtypeStruct((t, D_MODEL), F32),
        compiler_params=pltpu.CompilerParams(
            dimension_semantics=("parallel", "arbitrary"), vmem_limit_bytes=VMEM_LIMIT),
        name="ffn",
    )(h, x2, wg_bf, wu_bf, wd_bf, gain_final)


def kernel(x, norm_mix, w_in, attn_sinks, ssm_lam_re, ssm_lam_im, ssm_log_dt, ssm_b_re, ssm_b_im,
           ssm_c_re, ssm_c_im, ssm_d, ssm_w_glu, gmlp_ln_g, gmlp_ln_b, gmlp_w_s, gmlp_b_s,
           out_norm_attn, out_norm_ssm, out_norm_gmlp, w_out, norm_ffn, w_gate, w_up, w_down,
           norm_final):
    bsz, seq, _ = x.shape
    depth = w_in.shape[0]
    t = bsz * seq
    chunks_per_seq = seq // SSM_CHUNK
    assert seq % 1024 == 0 and (1 << SSM_SCAN_STEPS) == chunks_per_seq
    rows = t // SSM_CHUNK

    w_in_bf = w_in.astype(BF16)
    w_out_bf = w_out.astype(BF16)
    w_gate_bf = w_gate.astype(BF16)
    w_up_bf = w_up.astype(BF16)
    w_down_bf = w_down.astype(BF16)
    ssm_tbls = _ssm_tables(ssm_lam_re, ssm_lam_im, ssm_log_dt, ssm_b_re, ssm_b_im,
                           ssm_c_re, ssm_c_im, ssm_d)

    causal = jnp.tril(jnp.ones((GMLP_CHUNK, GMLP_CHUNK), dtype=bool))
    head_of = ((jnp.arange(N_KV_HEADS)[None, :, None] * 4 + (jnp.arange(4 * ATTN_BLOCK) // ATTN_BLOCK)[:, None, None]) * 2
               + jnp.arange(2)[None, None, :]).reshape(4 * ATTN_BLOCK, 2 * N_KV_HEADS)

    x2 = x.reshape(t, D_MODEL)
    for l in range(depth):
        wst = w_in[l][:, COL_SSM:COL_ZU].T.astype(BF16)
        q, kv, zu, zv, ut = _in_proj(x2, norm_mix[l][None], w_in_bf, wst, l, tm=1024)

        sink_cols = jnp.broadcast_to(
            (attn_sinks[l].astype(F32) * LOG2E)[head_of].T[:, :, None], (2 * N_KV_HEADS, 4 * ATTN_BLOCK, LANES))
        y_attn = _attention(q, kv, sink_cols, out_norm_attn[l][None], bsz, seq, nblk=4)

        u4 = ut.reshape(SSM_GROUPS, SSM_GROUP, rows, SSM_CHUNK)
        y4 = _ssm(u4, *ssm_tbls, l, chunks_per_seq)
        yt = y4.reshape(SSM_WIDTH, t)
        y_ssm = _ssm_glu(yt, ssm_w_glu[l].T.astype(BF16), out_norm_ssm[l][:, None], tm=512)

        ws = jnp.where(causal[None], gmlp_w_s[l], 0.0).astype(BF16)
        y_gmlp = _gmlp(zu, zv, gmlp_ln_g[l][None], gmlp_ln_b[l][None], ws, gmlp_b_s[l].T,
                       out_norm_gmlp[l][None], tm=512)

        x2, h = _out_proj(x2, y_attn, y_ssm, y_gmlp, w_out_bf, norm_ffn[l][None], l, tm=512)
        x2 = _ffn(h, x2, w_gate_bf, w_up_bf, w_down_bf, norm_final[None], l, tm=1024, tf=512,
                  final_norm=(l == depth - 1))
    return x2.reshape(bsz, seq, D_MODEL)
```

```python
import functools
import math

import jax
import jax.numpy as jnp
from jax import lax
from jax.experimental import pallas as pl
from jax.experimental.pallas import tpu as pltpu

F32 = jnp.float32
BF16 = jnp.bfloat16

D_MODEL = 2048
HEAD_DIM = 64
ATTN_WIDTH = 1024
N_Q_HEADS = ATTN_WIDTH // HEAD_DIM
N_KV_HEADS = 2
KV_WIDTH = N_KV_HEADS * HEAD_DIM
ATTN_BLOCK = 128
SSM_WIDTH = 512
SSM_GROUP = 16
SSM_GROUPS = SSM_WIDTH // SSM_GROUP
SSM_STATE = 64
GMLP_WIDTH = 512
GMLP_CHUNK = 128
GMLP_GROUPS = GMLP_WIDTH // 128
EPS = 1e-5
LOG2E = math.log2(math.e)
COL_KV = ATTN_WIDTH
COL_SSM = COL_KV + 2 * KV_WIDTH
COL_ZU = COL_SSM + SSM_WIDTH
COL_ZV = COL_ZU + GMLP_WIDTH
COL_END = COL_ZV + GMLP_WIDTH

LANES = 128
SUBLANES_F32 = 8
SSM_CHUNK = LANES
SSM_SCAN_STEPS = 7
VMEM_LIMIT = 56 * 1024 * 1024


def _gelu_tanh(x):
    c = math.sqrt(2.0 / math.pi)
    return 0.5 * x * (1.0 + jnp.tanh(c * (x + 0.044715 * (x * x * x))))


def _rms_lanes(y, gain):
    ms = jnp.mean(y * y, axis=-1, keepdims=True)
    return y * lax.rsqrt(ms + EPS) * gain


def _gmlp_mix(zu, zv, ln_g, ln_b, ws_ref, bs, gain):
    tm = zu.shape[0]
    u = _gelu_tanh(zu)
    v = _gelu_tanh(zv)
    vc = v - jnp.mean(v, axis=-1, keepdims=True)
    v = vc * lax.rsqrt(jnp.mean(vc * vc, axis=-1, keepdims=True) + EPS) * ln_g + ln_b
    vb = v.astype(BF16)
    cols = []
    for g in range(GMLP_GROUPS):
        w = ws_ref[g]
        parts = []
        for c in range(tm // GMLP_CHUNK):
            vblk = vb[c * GMLP_CHUNK:(c + 1) * GMLP_CHUNK, g * LANES:(g + 1) * LANES]
            parts.append(jnp.dot(w, vblk, preferred_element_type=F32) + bs[:, g:g + 1])
        cols.append(jnp.concatenate(parts, axis=0))
    mixed = jnp.concatenate(cols, axis=1)
    return _rms_lanes(u * mixed, gain)


def _in_proj_kernel(x_ref, g_ref, w_ref, wst_ref, lng_ref, lnb_ref, ws_ref, bs_ref, gg_ref,
                    q_ref, kv_ref, yg_ref, ut_ref):
    h = _rms_lanes(x_ref[...], g_ref[...]).astype(BF16)

    def proj(lo, hi):
        return jnp.dot(h, w_ref[:, lo:hi], preferred_element_type=F32)

    yg_ref[...] = _gmlp_mix(proj(COL_ZU, COL_ZV), proj(COL_ZV, COL_END), lng_ref[...], lnb_ref[...],
                            ws_ref, bs_ref[...], gg_ref[...]).astype(BF16)
    q_ref[...] = (proj(0, COL_KV) * (HEAD_DIM ** -0.5 * LOG2E)).astype(BF16)
    kv_ref[...] = proj(COL_KV, COL_SSM).astype(BF16)
    ut = lax.dot_general(wst_ref[...], h, (((1,), (1,)), ((), ())), preferred_element_type=F32)
    for c in range(ut_ref.shape[1]):
        ut_ref[:, c, :] = ut[:, c * SSM_CHUNK:(c + 1) * SSM_CHUNK]


def _in_proj(x2, gain, w_in_bf, wst, ln_g, ln_b, ws, bs_t, gain_gmlp, layer, tm):
    t = x2.shape[0]
    const = lambda i: (0, 0)
    row = lambda i: (i, 0)
    return pl.pallas_call(
        _in_proj_kernel,
        grid=(t // tm,),
        in_specs=[
            pl.BlockSpec((tm, D_MODEL), row),
            pl.BlockSpec((1, D_MODEL), const),
            pl.BlockSpec((None,) + w_in_bf.shape[1:], lambda i: (layer, 0, 0),
                         pipeline_mode=pl.Buffered(1)),
            pl.BlockSpec(wst.shape, const),
            pl.BlockSpec((1, GMLP_WIDTH), const),
            pl.BlockSpec((1, GMLP_WIDTH), const),
            pl.BlockSpec(ws.shape, lambda i: (0, 0, 0)),
            pl.BlockSpec(bs_t.shape, const),
            pl.BlockSpec((1, GMLP_WIDTH), const),
        ],
        out_specs=[
            pl.BlockSpec((tm, ATTN_WIDTH), row),
            pl.BlockSpec((tm, 2 * KV_WIDTH), row),
            pl.BlockSpec((tm, GMLP_WIDTH), row),
            pl.BlockSpec((SSM_WIDTH, tm // SSM_CHUNK, SSM_CHUNK), lambda i: (0, i, 0)),
        ],
        out_shape=[
            jax.ShapeDtypeStruct((t, ATTN_WIDTH), BF16),
            jax.ShapeDtypeStruct((t, 2 * KV_WIDTH), BF16),
            jax.ShapeDtypeStruct((t, GMLP_WIDTH), BF16),
            jax.ShapeDtypeStruct((SSM_WIDTH, t // SSM_CHUNK, SSM_CHUNK), F32),
        ],
        compiler_params=pltpu.CompilerParams(
            dimension_semantics=("parallel",), vmem_limit_bytes=VMEM_LIMIT),
        name="in_proj",
    )(x2, gain, w_in_bf, wst, ln_g, ln_b, ws, bs_t, gain_gmlp)


def _attn_kernel(q_ref, kvp_ref, kvc_ref, sink_ref, g_ref, o_ref, *, nblk):
    first_tile = pl.program_id(1) == 0
    lo = lax.broadcasted_iota(jnp.int32, (ATTN_BLOCK, LANES), 1) < HEAD_DIM

    def padded(kvblk):
        kvf = kvblk.astype(F32)
        kk, vv = kvf[:, :LANES], kvf[:, LANES:]
        kk_sw = pltpu.roll(kk, HEAD_DIM, axis=1)
        vv_sw = pltpu.roll(vv, HEAD_DIM, axis=1)
        ke = [jnp.where(lo, kk, 0.0), jnp.where(lo, kk_sw, 0.0)]
        ko = [jnp.where(lo, 0.0, kk_sw), jnp.where(lo, 0.0, kk)]
        ve = [jnp.where(lo, vv, 0.0), jnp.where(lo, vv_sw, 0.0)]
        vo = [jnp.where(lo, 0.0, vv_sw), jnp.where(lo, 0.0, vv)]
        return [[a.astype(BF16) for a in lst] for lst in (ke, ko, ve, vo)]

    keys = [padded(kvp_ref[...])] + [
        padded(kvc_ref[i * ATTN_BLOCK:(i + 1) * ATTN_BLOCK, :]) for i in range(nblk)]

    rows = 4 * ATTN_BLOCK
    qi = lax.broadcasted_iota(jnp.int32, (rows, LANES), 0) & (ATTN_BLOCK - 1)
    ci = lax.broadcasted_iota(jnp.int32, (rows, LANES), 1)
    vis_prev = ci > qi
    vis_prev_first = vis_prev & jnp.logical_not(first_tile)
    lo_rows = ci < HEAD_DIM
    even_key_rows = lax.broadcasted_iota(jnp.int32, (rows, LANES), 0) < 2 * ATTN_BLOCK
    ones = jnp.where(lo_rows == even_key_rows, 1.0, 0.0).astype(BF16)

    for i in range(nblk):
        q = q_ref[i * ATTN_BLOCK:(i + 1) * ATTN_BLOCK, :]
        vp = vis_prev_first if i == 0 else vis_prev
        outs = []
        for kh in range(N_KV_HEADS):
            kz = jnp.concatenate([keys[i][0][kh], keys[i + 1][0][kh],
                                  keys[i][1][kh], keys[i + 1][1][kh]], axis=0)
            vz = jnp.concatenate([keys[i][2][kh], keys[i + 1][2][kh],
                                  keys[i][3][kh], keys[i + 1][3][kh]], axis=0)
            qs = jnp.concatenate(
                [q[:, (kh * 4 + p) * LANES:(kh * 4 + p + 1) * LANES] for p in range(4)], axis=0)
            s = lax.dot_general(qs, kz, (((1,), (1,)), ((), ())), preferred_element_type=F32)
            es, ts = [], []
            for par in range(2):
                c0 = par * 2 * ATTN_BLOCK
                sp = jnp.where(vp, s[:, c0:c0 + ATTN_BLOCK], -jnp.inf)
                sc = jnp.where(vis_prev, -jnp.inf, s[:, c0 + ATTN_BLOCK:c0 + 2 * ATTN_BLOCK])
                sk = sink_ref[kh * 2 + par]
                m = jnp.maximum(jnp.max(jnp.maximum(sp, sc), axis=-1, keepdims=True), sk)
                es.append(jnp.exp2(sp - m).astype(BF16))
                es.append(jnp.exp2(sc - m).astype(BF16))
                ts.append(jnp.exp2(sk - m))
            e = jnp.concatenate(es, axis=1)
            od = jnp.dot(e, jnp.concatenate([vz, ones], axis=1), preferred_element_type=F32)
            den = od[:, LANES:] + jnp.where(lo_rows, ts[0], ts[1])
            o = od[:, :LANES] / den
            outs.extend([o[pp * ATTN_BLOCK:(pp + 1) * ATTN_BLOCK] for pp in range(4)])
        y = jnp.concatenate(outs, axis=1)
        o_ref[i * ATTN_BLOCK:(i + 1) * ATTN_BLOCK, :] = _rms_lanes(y, g_ref[...]).astype(BF16)


def _attention(q, kv, sink_cols, gain, bsz, seq, nblk):
    t = q.shape[0]
    tq = nblk * ATTN_BLOCK
    nt = seq // tq
    return pl.pallas_call(
        functools.partial(_attn_kernel, nblk=nblk),
        grid=(bsz, nt),
        in_specs=[
            pl.BlockSpec((tq, ATTN_WIDTH), lambda b, n: (b * nt + n, 0)),
            pl.BlockSpec((ATTN_BLOCK, 2 * KV_WIDTH),
                         lambda b, n: ((b * nt + n) * nblk - jnp.minimum(n, 1), 0)),
            pl.BlockSpec((tq, 2 * KV_WIDTH), lambda b, n: (b * nt + n, 0)),
            pl.BlockSpec(sink_cols.shape, lambda b, n: (0, 0, 0)),
            pl.BlockSpec((1, ATTN_WIDTH), lambda b, n: (0, 0)),
        ],
        out_specs=pl.BlockSpec((tq, ATTN_WIDTH), lambda b, n: (b * nt + n, 0)),
        out_shape=jax.ShapeDtypeStruct((t, ATTN_WIDTH), BF16),
        compiler_params=pltpu.CompilerParams(
            dimension_semantics=("parallel", "parallel"), vmem_limit_bytes=VMEM_LIMIT),
        name="swa_attention",
    )(q, kv, kv, sink_cols, gain)


def _ssm_kernel(u_ref, kfirst_ref, kodd_ref, knext_ref, e_ref, f_ref, ap_ref, y_ref, m_even, m_odd,
                *, chunks_per_seq):
    rows = u_ref.shape[2]
    tri = (lax.broadcasted_iota(jnp.int32, (LANES, LANES), 1)
           >= lax.broadcasted_iota(jnp.int32, (LANES, LANES), 0))

    def build(k_ref, m_ref):
        for hp in range(SSM_GROUP):
            for h in range(SSM_GROUP):
                k = k_ref[0, hp * SSM_GROUP + h:hp * SSM_GROUP + h + 1, :]
                kb = jnp.broadcast_to(k, (LANES, LANES))
                kr = pltpu.roll(kb, 0, 1, stride=1, stride_axis=0)
                m_ref[hp * LANES:(hp + 1) * LANES, h * LANES:(h + 1) * LANES] = (
                    jnp.where(tri, kr, 0.0).astype(BF16))

    cidx = lax.broadcasted_iota(jnp.int32, (rows, LANES), 0) & (chunks_per_seq - 1)

    def mix(slot, m_ref):
        u = jnp.concatenate([u_ref[slot, hp].astype(BF16) for hp in range(SSM_GROUP)], axis=1)
        y = jnp.dot(u, m_ref[...], preferred_element_type=F32)
        s = jnp.dot(u, e_ref[slot], preferred_element_type=F32)
        for j in range(SSM_SCAN_STEPS):
            d = 1 << j
            prev = jnp.where(cidx >= d, pltpu.roll(s, d, axis=0), 0.0)
            a_same = ap_ref[slot, 2 * j:2 * j + 1, :]
            a_cross = ap_ref[slot, 2 * j + 1:2 * j + 2, :]
            s = s + prev * a_same + pltpu.roll(prev, SSM_STATE, axis=1) * a_cross
        s_prev = jnp.where(cidx >= 1, pltpu.roll(s, 1, axis=0), 0.0)
        f = jnp.concatenate([f_ref[slot, h] for h in range(SSM_GROUP)], axis=1)
        y = y + jnp.dot(s_prev.astype(BF16), f, preferred_element_type=F32)
        for h in range(SSM_GROUP):
            y_ref[slot, h] = y[:, h * LANES:(h + 1) * LANES]

    @pl.when(pl.program_id(0) == 0)
    def _():
        build(kfirst_ref, m_even)

    mix(0, m_even)
    build(kodd_ref, m_odd)
    mix(1, m_odd)
    build(knext_ref, m_even)


def _ssm(u4, kc, e, f, ap, layer, chunks_per_seq):
    g, hh, rows, _ = u4.shape
    kblk = (None, 1) + kc.shape[2:]
    return pl.pallas_call(
        functools.partial(_ssm_kernel, chunks_per_seq=chunks_per_seq),
        grid=(g // 2,),
        in_specs=[
            pl.BlockSpec((2, hh, rows, LANES), lambda i: (i, 0, 0, 0)),
            pl.BlockSpec(kblk, lambda i: (layer, 0, 0, 0)),
            pl.BlockSpec(kblk, lambda i: (layer, 2 * i + 1, 0, 0)),
            pl.BlockSpec(kblk, lambda i: (layer, jnp.minimum(2 * i + 2, g - 1), 0, 0)),
            pl.BlockSpec((None, 2) + e.shape[2:], lambda i: (layer, i, 0, 0)),
            pl.BlockSpec((None, 2) + f.shape[2:], lambda i: (layer, i, 0, 0, 0)),
            pl.BlockSpec((None, 2) + ap.shape[2:], lambda i: (layer, i, 0, 0)),
        ],
        out_specs=pl.BlockSpec((2, hh, rows, LANES), lambda i: (i, 0, 0, 0)),
        out_shape=jax.ShapeDtypeStruct(u4.shape, F32),
        scratch_shapes=[pltpu.VMEM((SSM_GROUP * LANES, SSM_GROUP * LANES), BF16),
                        pltpu.VMEM((SSM_GROUP * LANES, SSM_GROUP * LANES), BF16)],
        compiler_params=pltpu.CompilerParams(
            dimension_semantics=("arbitrary",), vmem_limit_bytes=VMEM_LIMIT),
        name="s5_chunked",
    )(u4, kc, kc, kc, e, f, ap)


def _ssm_tables(lam_re, lam_im, log_dt, b_re, b_im, c_re, c_im, d_skip):
    hp = lax.Precision.HIGHEST
    nl = lam_re.shape[0]
    lr, li = lam_re.astype(F32), lam_im.astype(F32)
    dt = jnp.exp(log_dt.astype(F32))[..., None]
    mag = jnp.exp(lr * dt)
    ar, ai = mag * jnp.cos(li * dt), mag * jnp.sin(li * dt)
    den = lr * lr + li * li
    nre, nim = ar - 1.0, ai
    coef_re = (nre * lr + nim * li) / den
    coef_im = (nim * lr - nre * li) / den
    br, bi = b_re.astype(F32), b_im.astype(F32)
    bbr = coef_re[..., None] * br - coef_im[..., None] * bi
    bbi = coef_re[..., None] * bi + coef_im[..., None] * br
    crp = jnp.swapaxes(c_re.astype(F32), -1, -2)
    cip = jnp.swapaxes(c_im.astype(F32), -1, -2)
    n = jnp.arange(SSM_CHUNK + 1, dtype=F32)
    pmag = jnp.exp((lr * dt)[..., None] * n)
    pang = (li * dt)[..., None] * n
    pr, pi = pmag * jnp.cos(pang), pmag * jnp.sin(pang)
    wr = bbr[..., :, None] * crp[..., None, :] - bbi[..., :, None] * cip[..., None, :]
    wi = bbr[..., :, None] * cip[..., None, :] + bbi[..., :, None] * crp[..., None, :]
    taps = (jnp.einsum("lgpn,lgpab->lgabn", pr[..., :SSM_CHUNK], wr, precision=hp)
            - jnp.einsum("lgpn,lgpab->lgabn", pi[..., :SSM_CHUNK], wi, precision=hp))
    dskip = d_skip.astype(F32).reshape(nl, SSM_GROUPS, SSM_GROUP)
    lag0 = (jnp.arange(SSM_CHUNK) == 0).astype(F32)
    taps = taps + (jnp.eye(SSM_GROUP, dtype=F32) * dskip[:, :, None, :])[..., None] * lag0
    kc = taps.reshape(nl, SSM_GROUPS, SSM_GROUP * SSM_GROUP, SSM_CHUNK)
    prr = jnp.swapaxes(pr[..., :SSM_CHUNK][..., ::-1], -1, -2)
    pir = jnp.swapaxes(pi[..., :SSM_CHUNK][..., ::-1], -1, -2)
    prr2 = jnp.concatenate([prr, prr], axis=-1)[:, :, None]
    pir2 = jnp.concatenate([pir, pir], axis=-1)[:, :, None]
    bbr_t = jnp.swapaxes(bbr, -1, -2)
    bbi_t = jnp.swapaxes(bbi, -1, -2)
    b_same = jnp.concatenate([bbr_t, bbi_t], axis=-1)[:, :, :, None, :]
    b_cross = jnp.concatenate([-bbi_t, bbr_t], axis=-1)[:, :, :, None, :]
    e = (prr2 * b_same + pir2 * b_cross).astype(BF16).reshape(
        nl, SSM_GROUPS, SSM_GROUP * SSM_CHUNK, 2 * SSM_STATE)
    p1r2 = jnp.concatenate([pr[..., 1:], pr[..., 1:]], axis=2)[:, :, None]
    p1i2 = jnp.concatenate([pi[..., 1:], pi[..., 1:]], axis=2)[:, :, None]
    cr_t = jnp.swapaxes(crp, -1, -2)
    ci_t = jnp.swapaxes(cip, -1, -2)
    c_same = jnp.concatenate([cr_t, -ci_t], axis=-1)[..., None]
    c_cross = jnp.concatenate([-ci_t, -cr_t], axis=-1)[..., None]
    f = (c_same * p1r2 + c_cross * p1i2).astype(BF16)
    a_r, a_i = pr[..., SSM_CHUNK], pi[..., SSM_CHUNK]
    rows = []
    for _ in range(SSM_SCAN_STEPS):
        rows.append(jnp.concatenate([a_r, a_r], axis=-1))
        rows.append(jnp.concatenate([-a_i, a_i], axis=-1))
        a_r, a_i = a_r * a_r - a_i * a_i, 2.0 * a_r * a_i
    rows.extend([jnp.zeros_like(rows[0])] * (16 - len(rows)))
    ap = jnp.stack(rows, axis=2)
    return kc, e, f, ap


def _out_proj_kernel(y3_ref, wglu_ref, gs_ref, x_ref, ya_ref, yg_ref, w_ref, g_ref, o_ref, h_ref):
    nchunk = x_ref.shape[0] // SSM_CHUNK
    base = (pl.program_id(0) % (y3_ref.shape[1] // nchunk)) * nchunk
    parts = []
    for c in range(0, nchunk, 2):
        y = _gelu_tanh(jnp.concatenate([y3_ref[:, base + c, :], y3_ref[:, base + c + 1, :]], axis=1))
        z = jnp.dot(wglu_ref[...], y.astype(BF16), preferred_element_type=F32)
        o = y * jax.nn.sigmoid(z)
        ms = jnp.mean(o * o, axis=0, keepdims=True)
        o = o * lax.rsqrt(ms + EPS) * gs_ref[...]
        parts.append(o.T.astype(BF16))
    ys = jnp.concatenate(parts, axis=0)
    y = jnp.concatenate([ya_ref[...], ys, yg_ref[...]], axis=1)
    xn = x_ref[...] + jnp.dot(y, w_ref[...], preferred_element_type=F32)
    o_ref[...] = xn
    h_ref[...] = _rms_lanes(xn, g_ref[...]).astype(BF16)


def _out_proj(y3, w_glu_t, gain_ssm_col, x2, ya, yg, w_out_bf, gain_ffn, layer, tm):
    t = x2.shape[0]
    row = lambda i: (i, 0)
    const = lambda i: (0, 0)
    slab_tiles = SUBLANES_F32 * SSM_CHUNK // tm
    return pl.pallas_call(
        _out_proj_kernel,
        grid=(t // tm,),
        in_specs=[
            pl.BlockSpec((SSM_WIDTH, SUBLANES_F32, SSM_CHUNK), lambda i: (0, i // slab_tiles, 0)),
            pl.BlockSpec((SSM_WIDTH, SSM_WIDTH), const),
            pl.BlockSpec((SSM_WIDTH, 1), const),
            pl.BlockSpec((tm, D_MODEL), row),
            pl.BlockSpec((tm, ATTN_WIDTH), row),
            pl.BlockSpec((tm, GMLP_WIDTH), row),
            pl.BlockSpec((None,) + w_out_bf.shape[1:], lambda i: (layer, 0, 0),
                         pipeline_mode=pl.Buffered(1)),
            pl.BlockSpec((1, D_MODEL), const),
        ],
        out_specs=[pl.BlockSpec((tm, D_MODEL), row), pl.BlockSpec((tm, D_MODEL), row)],
        out_shape=[jax.ShapeDtypeStruct((t, D_MODEL), F32), jax.ShapeDtypeStruct((t, D_MODEL), BF16)],
        compiler_params=pltpu.CompilerParams(
            dimension_semantics=("parallel",), vmem_limit_bytes=VMEM_LIMIT),
        name="out_proj",
    )(y3, w_glu_t, gain_ssm_col, x2, ya, yg, w_out_bf, gain_ffn)


FFN_RES_CHUNK = 256
FFN_RES_STEPS = D_MODEL // FFN_RES_CHUNK


def _ffn_kernel(h_ref, xc_ref, wg_ref, wu_ref, wd_ref, gf_ref, o_ref, *, final_norm, halves):
    j = pl.program_id(1)

    @pl.when(j == 0)
    def _():
        o_ref[...] = jnp.zeros_like(o_ref)

    @pl.when(j < FFN_RES_STEPS)
    def _():
        cols = pl.ds(pl.multiple_of(j * FFN_RES_CHUNK, FFN_RES_CHUNK), FFN_RES_CHUNK)
        o_ref[:, cols] += xc_ref[...]

    h = h_ref[...]
    hw = wg_ref.shape[1] // halves
    acc = None
    for c in range(halves):
        gate = jnp.dot(h, wg_ref[:, c * hw:(c + 1) * hw], preferred_element_type=F32)
        up = jnp.dot(h, wu_ref[:, c * hw:(c + 1) * hw], preferred_element_type=F32)
        a = (gate * jax.nn.sigmoid(gate) * up).astype(BF16)
        d = jnp.dot(a, wd_ref[c * hw:(c + 1) * hw, :], preferred_element_type=F32)
        acc = d if acc is None else acc + d
    o_ref[...] += acc

    if final_norm:
        @pl.when(j == pl.num_programs(1) - 1)
        def _():
            o_ref[...] = _rms_lanes(o_ref[...], gf_ref[...])


def _ffn(h, x2, wg_bf, wu_bf, wd_bf, gain_final, layer, tm, tf, final_norm):
    t = x2.shape[0]
    d_ff = wg_bf.shape[2]
    assert d_ff // tf >= FFN_RES_STEPS
    return pl.pallas_call(
        functools.partial(_ffn_kernel, final_norm=final_norm, halves=2),
        grid=(t // tm, d_ff // tf),
        in_specs=[
            pl.BlockSpec((tm, D_MODEL), lambda i, j: (i, 0)),
            pl.BlockSpec((tm, FFN_RES_CHUNK), lambda i, j: (i, jnp.minimum(j, FFN_RES_STEPS - 1))),
            pl.BlockSpec((None, D_MODEL, tf), lambda i, j: (layer, 0, j)),
            pl.BlockSpec((None, D_MODEL, tf), lambda i, j: (layer, 0, j)),
            pl.BlockSpec((None, tf, D_MODEL), lambda i, j: (layer, j, 0)),
            pl.BlockSpec((1, D_MODEL), lambda i, j: (0, 0)),
        ],
        out_specs=pl.BlockSpec((tm, D_MODEL), lambda i, j: (i, 0)),
        out_shape=jax.ShapeDtypeStruct((t, D_MODEL), F32),
        compiler_params=pltpu.CompilerParams(
            dimension_semantics=("parallel", "arbitrary"), vmem_limit_bytes=VMEM_LIMIT),
        name="ffn",
    )(h, x2, wg_bf, wu_bf, wd_bf, gain_final)


def kernel(x, norm_mix, w_in, attn_sinks, ssm_lam_re, ssm_lam_im, ssm_log_dt, ssm_b_re, ssm_b_im,
           ssm_c_re, ssm_c_im, ssm_d, ssm_w_glu, gmlp_ln_g, gmlp_ln_b, gmlp_w_s, gmlp_b_s,
           out_norm_attn, out_norm_ssm, out_norm_gmlp, w_out, norm_ffn, w_gate, w_up, w_down,
           norm_final):
    bsz, seq, _ = x.shape
    depth = w_in.shape[0]
    t = bsz * seq
    chunks_per_seq = seq // SSM_CHUNK
    assert seq % 1024 == 0 and (1 << SSM_SCAN_STEPS) == chunks_per_seq
    rows = t // SSM_CHUNK

    w_in_bf = w_in.astype(BF16)
    w_out_bf = w_out.astype(BF16)
    w_gate_bf = w_gate.astype(BF16)
    w_up_bf = w_up.astype(BF16)
    w_down_bf = w_down.astype(BF16)
    ssm_tbls = _ssm_tables(ssm_lam_re, ssm_lam_im, ssm_log_dt, ssm_b_re, ssm_b_im,
                           ssm_c_re, ssm_c_im, ssm_d)

    causal = jnp.tril(jnp.ones((GMLP_CHUNK, GMLP_CHUNK), dtype=bool))
    head_of = ((jnp.arange(N_KV_HEADS)[None, :, None] * 4 + (jnp.arange(4 * ATTN_BLOCK) // ATTN_BLOCK)[:, None, None]) * 2
               + jnp.arange(2)[None, None, :]).reshape(4 * ATTN_BLOCK, 2 * N_KV_HEADS)

    x2 = x.reshape(t, D_MODEL)
    for l in range(depth):
        wst = w_in[l][:, COL_SSM:COL_ZU].T.astype(BF16)
        ws = jnp.where(causal[None], gmlp_w_s[l], 0.0).astype(BF16)
        q, kv, y_gmlp, ut = _in_proj(x2, norm_mix[l][None], w_in_bf, wst, gmlp_ln_g[l][None],
                                     gmlp_ln_b[l][None], ws, gmlp_b_s[l].T, out_norm_gmlp[l][None],
                                     l, tm=1024)

        sink_cols = jnp.broadcast_to(
            (attn_sinks[l].astype(F32) * LOG2E)[head_of].T[:, :, None], (2 * N_KV_HEADS, 4 * ATTN_BLOCK, LANES))
        y_attn = _attention(q, kv, sink_cols, out_norm_attn[l][None], bsz, seq, nblk=4)

        u4 = ut.reshape(SSM_GROUPS, SSM_GROUP, rows, SSM_CHUNK)
        y4 = _ssm(u4, *ssm_tbls, l, chunks_per_seq)
        y3 = y4.reshape(SSM_WIDTH, rows, SSM_CHUNK)

        x2, h = _out_proj(y3, ssm_w_glu[l].T.astype(BF16), out_norm_ssm[l][:, None], x2, y_attn, y_gmlp,
                          w_out_bf, norm_ffn[l][None], l, tm=512)
        x2 = _ffn(h, x2, w_gate_bf, w_up_bf, w_down_bf, norm_final[None], l, tm=1024, tf=512,
                  final_norm=(l == depth - 1))
    return x2.reshape(bsz, seq, D_MODEL)
```

```python
import functools
import math

import jax
import jax.numpy as jnp
from jax import lax
from jax.experimental import pallas as pl
from jax.experimental.pallas import tpu as pltpu

F32 = jnp.float32
BF16 = jnp.bfloat16

D_MODEL = 2048
HEAD_DIM = 64
ATTN_WIDTH = 1024
N_Q_HEADS = ATTN_WIDTH // HEAD_DIM
N_KV_HEADS = 2
KV_WIDTH = N_KV_HEADS * HEAD_DIM
ATTN_BLOCK = 128
SSM_WIDTH = 512
SSM_GROUP = 16
SSM_GROUPS = SSM_WIDTH // SSM_GROUP
SSM_STATE = 64
GMLP_WIDTH = 512
GMLP_CHUNK = 128
GMLP_GROUPS = GMLP_WIDTH // 128
EPS = 1e-5
LOG2E = math.log2(math.e)
COL_KV = ATTN_WIDTH
COL_SSM = COL_KV + 2 * KV_WIDTH
COL_ZU = COL_SSM + SSM_WIDTH
COL_ZV = COL_ZU + GMLP_WIDTH
COL_END = COL_ZV + GMLP_WIDTH

LANES = 128
SUBLANES_F32 = 8
SSM_CHUNK = LANES
SSM_SCAN_STEPS = 7
GLU_SLABS = 2
VMEM_LIMIT = 56 * 1024 * 1024


def _gelu_tanh(x):
    k = -2.0 * math.sqrt(2.0 / math.pi) * LOG2E
    return x / (1.0 + jnp.exp2(x * ((x * x) * (0.044715 * k) + k)))


def _cast_kernel(x_ref, o_ref):
    o_ref[...] = x_ref[...].astype(o_ref.dtype)


def _to_bf16(w, rows):
    nl, r, c = w.shape
    return pl.pallas_call(
        _cast_kernel,
        grid=(nl, r // rows),
        in_specs=[pl.BlockSpec((None, rows, c), lambda l, i: (l, i, 0))],
        out_specs=pl.BlockSpec((None, rows, c), lambda l, i: (l, i, 0)),
        out_shape=jax.ShapeDtypeStruct(w.shape, BF16),
        compiler_params=pltpu.CompilerParams(
            dimension_semantics=("parallel", "parallel"), vmem_limit_bytes=VMEM_LIMIT),
        name="to_bf16",
    )(w)


def _rms_lanes(y, gain):
    ms = jnp.mean(y * y, axis=-1, keepdims=True)
    return y * lax.rsqrt(ms + EPS) * gain


def _transpose_cast_kernel(x_ref, o_ref):
    o_ref[...] = x_ref[...].T.astype(o_ref.dtype)


def _ssm_weight_t(w_in, layer, cols):
    assert COL_SSM % cols == 0 and SSM_WIDTH % cols == 0
    return pl.pallas_call(
        _transpose_cast_kernel,
        grid=(SSM_WIDTH // cols,),
        in_specs=[pl.BlockSpec((None, D_MODEL, cols), lambda i: (layer, 0, COL_SSM // cols + i))],
        out_specs=pl.BlockSpec((cols, D_MODEL), lambda i: (i, 0)),
        out_shape=jax.ShapeDtypeStruct((SSM_WIDTH, D_MODEL), BF16),
        compiler_params=pltpu.CompilerParams(
            dimension_semantics=("parallel",), vmem_limit_bytes=VMEM_LIMIT),
        name="ssm_weight_t",
    )(w_in)


def _gmlp_mix(zu, zv, ln_g, ln_b, ws_ref, bs, gain):
    tm = zu.shape[0]
    u = _gelu_tanh(zu)
    v = _gelu_tanh(zv)
    vc = v - jnp.mean(v, axis=-1, keepdims=True)
    v = vc * lax.rsqrt(jnp.mean(vc * vc, axis=-1, keepdims=True) + EPS) * ln_g + ln_b
    vb = v.astype(BF16)
    cols = []
    for g in range(GMLP_GROUPS):
        w = ws_ref[g]
        parts = []
        for c in range(tm // GMLP_CHUNK):
            vblk = vb[c * GMLP_CHUNK:(c + 1) * GMLP_CHUNK, g * LANES:(g + 1) * LANES]
            parts.append(jnp.dot(w, vblk, preferred_element_type=F32) + bs[:, g:g + 1])
        cols.append(jnp.concatenate(parts, axis=0))
    mixed = jnp.concatenate(cols, axis=1)
    return _rms_lanes(u * mixed, gain)


def _in_proj_kernel(x_ref, g_ref, w_ref, wst_ref, lng_ref, lnb_ref, ws_ref, bs_ref, gg_ref,
                    q_ref, kv_ref, yg_ref, ut_ref):
    h = _rms_lanes(x_ref[...], g_ref[...]).astype(BF16)

    def proj(lo, hi):
        return jnp.dot(h, w_ref[:, lo:hi], preferred_element_type=F32)

    zu = proj(COL_ZU, COL_ZV)
    zv = proj(COL_ZV, COL_END)
    parts = 4
    rm = x_ref.shape[0] // parts
    qw = ATTN_WIDTH // parts
    for p in range(parts):
        rows = slice(p * rm, (p + 1) * rm)
        yg_ref[rows, :] = _gmlp_mix(zu[rows], zv[rows], lng_ref[...], lnb_ref[...],
                                    ws_ref, bs_ref[...], gg_ref[...]).astype(BF16)
        q_ref[:, p * qw:(p + 1) * qw] = (proj(p * qw, (p + 1) * qw) * (HEAD_DIM ** -0.5 * LOG2E)).astype(BF16)
    kv_ref[...] = proj(COL_KV, COL_SSM).astype(BF16)
    ut = lax.dot_general(wst_ref[...], h, (((1,), (1,)), ((), ())), preferred_element_type=F32)
    for c in range(ut_ref.shape[2]):
        ut_ref[:, :, c, :] = ut[:, c * SSM_CHUNK:(c + 1) * SSM_CHUNK].reshape(
            SSM_GROUPS, SSM_GROUP, SSM_CHUNK)


def _in_proj(x2, gain, w_in_bf, wst, ln_g, ln_b, ws, bs_t, gain_gmlp, layer, tm):
    t = x2.shape[0]
    const = lambda i: (0, 0)
    row = lambda i: (i, 0)
    return pl.pallas_call(
        _in_proj_kernel,
        grid=(t // tm,),
        in_specs=[
            pl.BlockSpec((tm, D_MODEL), row),
            pl.BlockSpec((1, D_MODEL), const),
            pl.BlockSpec((None,) + w_in_bf.shape[1:], lambda i: (layer, 0, 0),
                         pipeline_mode=pl.Buffered(1)),
            pl.BlockSpec(wst.shape, const),
            pl.BlockSpec((1, GMLP_WIDTH), const),
            pl.BlockSpec((1, GMLP_WIDTH), const),
            pl.BlockSpec(ws.shape, lambda i: (0, 0, 0)),
            pl.BlockSpec(bs_t.shape, const),
            pl.BlockSpec((1, GMLP_WIDTH), const),
        ],
        out_specs=[
            pl.BlockSpec((tm, ATTN_WIDTH), row),
            pl.BlockSpec((tm, 2 * KV_WIDTH), row),
            pl.BlockSpec((tm, GMLP_WIDTH), row),
            pl.BlockSpec((SSM_GROUPS, SSM_GROUP, tm // SSM_CHUNK, SSM_CHUNK), lambda i: (0, 0, i, 0)),
        ],
        out_shape=[
            jax.ShapeDtypeStruct((t, ATTN_WIDTH), BF16),
            jax.ShapeDtypeStruct((t, 2 * KV_WIDTH), BF16),
            jax.ShapeDtypeStruct((t, GMLP_WIDTH), BF16),
            jax.ShapeDtypeStruct((SSM_GROUPS, SSM_GROUP, t // SSM_CHUNK, SSM_CHUNK), F32),
        ],
        compiler_params=pltpu.CompilerParams(
            dimension_semantics=("parallel",), vmem_limit_bytes=VMEM_LIMIT),
        name="in_proj",
    )(x2, gain, w_in_bf, wst, ln_g, ln_b, ws, bs_t, gain_gmlp)


def _attn_kernel(q_ref, kvp_ref, kvc_ref, sink_ref, g_ref, o_ref, *, nblk):
    first_tile = pl.program_id(1) == 0
    lo = lax.broadcasted_iota(jnp.int32, (ATTN_BLOCK, LANES), 1) < HEAD_DIM

    def padded(kvblk):
        kvf = kvblk.astype(F32)
        kk, vv = kvf[:, :LANES], kvf[:, LANES:]
        kk_sw = pltpu.roll(kk, HEAD_DIM, axis=1)
        vv_sw = pltpu.roll(vv, HEAD_DIM, axis=1)
        ke = [jnp.where(lo, kk, 0.0), jnp.where(lo, kk_sw, 0.0)]
        ko = [jnp.where(lo, 0.0, kk_sw), jnp.where(lo, 0.0, kk)]
        ve = [jnp.where(lo, vv, 0.0), jnp.where(lo, vv_sw, 0.0)]
        vo = [jnp.where(lo, 0.0, vv_sw), jnp.where(lo, 0.0, vv)]
        return [[a.astype(BF16) for a in lst] for lst in (ke, ko, ve, vo)]

    keys = [padded(kvp_ref[...])] + [
        padded(kvc_ref[i * ATTN_BLOCK:(i + 1) * ATTN_BLOCK, :]) for i in range(nblk)]

    rows = 4 * ATTN_BLOCK
    qi = lax.broadcasted_iota(jnp.int32, (rows, LANES), 0) & (ATTN_BLOCK - 1)
    ci = lax.broadcasted_iota(jnp.int32, (rows, LANES), 1)
    vis_prev = ci > qi
    vis_prev_first = vis_prev & jnp.logical_not(first_tile)
    lo_rows = ci < HEAD_DIM
    even_key_rows = lax.broadcasted_iota(jnp.int32, (rows, LANES), 0) < 2 * ATTN_BLOCK
    ones = jnp.where(lo_rows == even_key_rows, 1.0, 0.0).astype(BF16)

    for i in range(nblk):
        q = q_ref[i * ATTN_BLOCK:(i + 1) * ATTN_BLOCK, :]
        vp = vis_prev_first if i == 0 else vis_prev
        outs = []
        for kh in range(N_KV_HEADS):
            kz = jnp.concatenate([keys[i][0][kh], keys[i + 1][0][kh],
                                  keys[i][1][kh], keys[i + 1][1][kh]], axis=0)
            vz = jnp.concatenate([keys[i][2][kh], keys[i + 1][2][kh],
                                  keys[i][3][kh], keys[i + 1][3][kh]], axis=0)
            qs = jnp.concatenate(
                [q[:, (kh * 4 + p) * LANES:(kh * 4 + p + 1) * LANES] for p in range(4)], axis=0)
            s = lax.dot_general(qs, kz, (((1,), (1,)), ((), ())), preferred_element_type=F32)
            es, ts = [], []
            for par in range(2):
                c0 = par * 2 * ATTN_BLOCK
                sp = jnp.where(vp, s[:, c0:c0 + ATTN_BLOCK], -jnp.inf)
                sc = jnp.where(vis_prev, -jnp.inf, s[:, c0 + ATTN_BLOCK:c0 + 2 * ATTN_BLOCK])
                sk = sink_ref[kh * 2 + par]
                m = jnp.maximum(jnp.max(jnp.maximum(sp, sc), axis=-1, keepdims=True), sk)
                es.append(jnp.exp2(sp - m).astype(BF16))
                es.append(jnp.exp2(sc - m).astype(BF16))
                ts.append(jnp.exp2(sk - m))
            e = jnp.concatenate(es, axis=1)
            od = jnp.dot(e, jnp.concatenate([vz, ones], axis=1), preferred_element_type=F32)
            den = od[:, LANES:] + jnp.where(lo_rows, ts[0], ts[1])
            o = od[:, :LANES] / den
            outs.extend([o[pp * ATTN_BLOCK:(pp + 1) * ATTN_BLOCK] for pp in range(4)])
        y = jnp.concatenate(outs, axis=1)
        o_ref[i * ATTN_BLOCK:(i + 1) * ATTN_BLOCK, :] = _rms_lanes(y, g_ref[...]).astype(BF16)


def _attention(q, kv, sink_cols, gain, bsz, seq, nblk):
    t = q.shape[0]
    tq = nblk * ATTN_BLOCK
    nt = seq // tq
    return pl.pallas_call(
        functools.partial(_attn_kernel, nblk=nblk),
        grid=(bsz, nt),
        in_specs=[
            pl.BlockSpec((tq, ATTN_WIDTH), lambda b, n: (b * nt + n, 0)),
            pl.BlockSpec((ATTN_BLOCK, 2 * KV_WIDTH),
                         lambda b, n: ((b * nt + n) * nblk - jnp.minimum(n, 1), 0)),
            pl.BlockSpec((tq, 2 * KV_WIDTH), lambda b, n: (b * nt + n, 0)),
            pl.BlockSpec(sink_cols.shape, lambda b, n: (0, 0, 0)),
            pl.BlockSpec((1, ATTN_WIDTH), lambda b, n: (0, 0)),
        ],
        out_specs=pl.BlockSpec((tq, ATTN_WIDTH), lambda b, n: (b * nt + n, 0)),
        out_shape=jax.ShapeDtypeStruct((t, ATTN_WIDTH), BF16),
        compiler_params=pltpu.CompilerParams(
            dimension_semantics=("parallel", "parallel"), vmem_limit_bytes=VMEM_LIMIT),
        name="swa_attention",
    )(q, kv, kv, sink_cols, gain)


def _ssm_kernel(u_ref, kfirst_ref, kodd_ref, knext_ref, pw_ref, bc_ref, cc_ref, ap_ref, y_ref,
                m_even, m_odd, *, chunks_per_seq):
    rows = u_ref.shape[2]
    tri = (lax.broadcasted_iota(jnp.int32, (LANES, LANES), 1)
           >= lax.broadcasted_iota(jnp.int32, (LANES, LANES), 0))

    def build(k_ref, m_ref):
        for hp in range(SSM_GROUP):
            for h in range(SSM_GROUP):
                k = k_ref[0, hp * SSM_GROUP + h:hp * SSM_GROUP + h + 1, :]
                kb = jnp.broadcast_to(k, (LANES, LANES))
                kr = pltpu.roll(kb, 0, 1, stride=1, stride_axis=0)
                m_ref[hp * LANES:(hp + 1) * LANES, h * LANES:(h + 1) * LANES] = (
                    jnp.where(tri, kr, 0.0).astype(BF16))

    cidx = lax.broadcasted_iota(jnp.int32, (rows, LANES), 0) & (chunks_per_seq - 1)

    def mix(slot, m_ref):
        u = jnp.concatenate([u_ref[slot, hp].astype(BF16) for hp in range(SSM_GROUP)], axis=1)
        y = jnp.dot(u, m_ref[...], preferred_element_type=F32)
        e = jnp.concatenate(
            [(pw_ref[slot, 0] * bc_ref[slot, 0, hp:hp + 1, :]
              + pw_ref[slot, 1] * bc_ref[slot, 1, hp:hp + 1, :]).astype(BF16)
             for hp in range(SSM_GROUP)], axis=0)
        s = jnp.dot(u, e, preferred_element_type=F32)
        for j in range(SSM_SCAN_STEPS):
            d = 1 << j
            prev = jnp.where(cidx >= d, pltpu.roll(s, d, axis=0), 0.0)
            a_same = ap_ref[slot, 2 * j:2 * j + 1, :]
            a_cross = ap_ref[slot, 2 * j + 1:2 * j + 2, :]
            s = s + prev * a_same + pltpu.roll(prev, SSM_STATE, axis=1) * a_cross
        s_prev = jnp.where(cidx >= 1, pltpu.roll(s, 1, axis=0), 0.0)
        f = jnp.concatenate(
            [(cc_ref[slot, :, h:h + 1] * pw_ref[slot, 2]
              + cc_ref[slot, :, SSM_GROUP + h:SSM_GROUP + h + 1] * pw_ref[slot, 3]).astype(BF16)
             for h in range(SSM_GROUP)], axis=1)
        y = y + jnp.dot(s_prev.astype(BF16), f, preferred_element_type=F32)
        for h in range(SSM_GROUP):
            y_ref[slot, h] = y[:, h * LANES:(h + 1) * LANES]

    @pl.when(pl.program_id(0) == 0)
    def _():
        build(kfirst_ref, m_even)

    mix(0, m_even)
    build(kodd_ref, m_odd)
    mix(1, m_odd)
    build(knext_ref, m_even)


def _ssm(u4, kc, pw, bc, cc, ap, layer, chunks_per_seq):
    g, hh, rows, _ = u4.shape
    kblk = (None, 1) + kc.shape[2:]
    return pl.pallas_call(
        functools.partial(_ssm_kernel, chunks_per_seq=chunks_per_seq),
        grid=(g // 2,),
        in_specs=[
            pl.BlockSpec((2, hh, rows, LANES), lambda i: (i, 0, 0, 0)),
            pl.BlockSpec(kblk, lambda i: (layer, 0, 0, 0)),
            pl.BlockSpec(kblk, lambda i: (layer, 2 * i + 1, 0, 0)),
            pl.BlockSpec(kblk, lambda i: (layer, jnp.minimum(2 * i + 2, g - 1), 0, 0)),
            pl.BlockSpec((None, 2) + pw.shape[2:], lambda i: (layer, i, 0, 0, 0)),
            pl.BlockSpec((None, 2) + bc.shape[2:], lambda i: (layer, i, 0, 0, 0)),
            pl.BlockSpec((None, 2) + cc.shape[2:], lambda i: (layer, i, 0, 0)),
            pl.BlockSpec((None, 2) + ap.shape[2:], lambda i: (layer, i, 0, 0)),
        ],
        out_specs=pl.BlockSpec((2, hh, rows, LANES), lambda i: (i, 0, 0, 0)),
        out_shape=jax.ShapeDtypeStruct(u4.shape, F32),
        scratch_shapes=[pltpu.VMEM((SSM_GROUP * LANES, SSM_GROUP * LANES), BF16),
                        pltpu.VMEM((SSM_GROUP * LANES, SSM_GROUP * LANES), BF16)],
        compiler_params=pltpu.CompilerParams(
            dimension_semantics=("arbitrary",), vmem_limit_bytes=VMEM_LIMIT),
        name="s5_chunked",
    )(u4, kc, kc, kc, pw, bc, cc, ap)


def _ssm_tables(lam_re, lam_im, log_dt, b_re, b_im, c_re, c_im, d_skip):
    hp = lax.Precision.HIGHEST
    nl = lam_re.shape[0]
    lr, li = lam_re.astype(F32), lam_im.astype(F32)
    dt = jnp.exp(log_dt.astype(F32))[..., None]
    mag = jnp.exp(lr * dt)
    ar, ai = mag * jnp.cos(li * dt), mag * jnp.sin(li * dt)
    den = lr * lr + li * li
    nre, nim = ar - 1.0, ai
    coef_re = (nre * lr + nim * li) / den
    coef_im = (nim * lr - nre * li) / den
    br, bi = b_re.astype(F32), b_im.astype(F32)
    bbr = coef_re[..., None] * br - coef_im[..., None] * bi
    bbi = coef_re[..., None] * bi + coef_im[..., None] * br
    crp = jnp.swapaxes(c_re.astype(F32), -1, -2)
    cip = jnp.swapaxes(c_im.astype(F32), -1, -2)
    n = jnp.arange(SSM_CHUNK + 1, dtype=F32)
    pmag = jnp.exp((lr * dt)[..., None] * n)
    pang = (li * dt)[..., None] * n
    pr, pi = pmag * jnp.cos(pang), pmag * jnp.sin(pang)
    wr = bbr[..., :, None] * crp[..., None, :] - bbi[..., :, None] * cip[..., None, :]
    wi = bbr[..., :, None] * cip[..., None, :] + bbi[..., :, None] * crp[..., None, :]
    taps = (jnp.einsum("lgpn,lgpab->lgabn", pr[..., :SSM_CHUNK], wr, precision=hp)
            - jnp.einsum("lgpn,lgpab->lgabn", pi[..., :SSM_CHUNK], wi, precision=hp))
    dskip = d_skip.astype(F32).reshape(nl, SSM_GROUPS, SSM_GROUP)
    lag0 = (jnp.arange(SSM_CHUNK) == 0).astype(F32)
    taps = taps + (jnp.eye(SSM_GROUP, dtype=F32) * dskip[:, :, None, :])[..., None] * lag0
    kc = taps.reshape(nl, SSM_GROUPS, SSM_GROUP * SSM_GROUP, SSM_CHUNK)
    prr = jnp.swapaxes(pr[..., :SSM_CHUNK][..., ::-1], -1, -2)
    pir = jnp.swapaxes(pi[..., :SSM_CHUNK][..., ::-1], -1, -2)
    pw = jnp.stack([jnp.concatenate([prr, prr], axis=-1),
                    jnp.concatenate([pir, pir], axis=-1),
                    jnp.concatenate([pr[..., 1:], pr[..., 1:]], axis=2),
                    jnp.concatenate([pi[..., 1:], pi[..., 1:]], axis=2)], axis=2)
    bbr_t = jnp.swapaxes(bbr, -1, -2)
    bbi_t = jnp.swapaxes(bbi, -1, -2)
    bc = jnp.stack([jnp.concatenate([bbr_t, bbi_t], axis=-1),
                    jnp.concatenate([-bbi_t, bbr_t], axis=-1)], axis=2)
    cc = jnp.concatenate([jnp.concatenate([crp, -cip], axis=2),
                          jnp.concatenate([-cip, -crp], axis=2)], axis=-1)
    a_r, a_i = pr[..., SSM_CHUNK], pi[..., SSM_CHUNK]
    rows = []
    for _ in range(SSM_SCAN_STEPS):
        rows.append(jnp.concatenate([a_r, a_r], axis=-1))
        rows.append(jnp.concatenate([-a_i, a_i], axis=-1))
        a_r, a_i = a_r * a_r - a_i * a_i, 2.0 * a_r * a_i
    rows.extend([jnp.zeros_like(rows[0])] * (16 - len(rows)))
    ap = jnp.stack(rows, axis=2)
    return kc, pw, bc, cc, ap


def _glu_kernel(y4_ref, wt_ref, g_ref, o_ref):
    def slab(c):
        return y4_ref[:, :, c, :].reshape(SSM_WIDTH, SSM_CHUNK)

    for c in range(0, y4_ref.shape[2], GLU_SLABS):
        y = _gelu_tanh(jnp.concatenate([slab(c + i) for i in range(GLU_SLABS)], axis=1))
        z = jnp.dot(wt_ref[...], y.astype(BF16), preferred_element_type=F32)
        o = y * jax.nn.sigmoid(z)
        ms = jnp.mean(o * o, axis=0, keepdims=True)
        o = o * lax.rsqrt(ms + EPS) * g_ref[...]
        o_ref[c * SSM_CHUNK:(c + GLU_SLABS) * SSM_CHUNK, :] = o.T.astype(BF16)


def _ssm_glu(y4, w_glu_t, gain_col, tm):
    t = y4.shape[2] * SSM_CHUNK
    return pl.pallas_call(
        _glu_kernel,
        grid=(t // tm,),
        in_specs=[
            pl.BlockSpec((SSM_GROUPS, SSM_GROUP, tm // SSM_CHUNK, SSM_CHUNK), lambda i: (0, 0, i, 0)),
            pl.BlockSpec((SSM_WIDTH, SSM_WIDTH), lambda i: (0, 0)),
            pl.BlockSpec((SSM_WIDTH, 1), lambda i: (0, 0)),
        ],
        out_specs=pl.BlockSpec((tm, SSM_WIDTH), lambda i: (i, 0)),
        out_shape=jax.ShapeDtypeStruct((t, SSM_WIDTH), BF16),
        compiler_params=pltpu.CompilerParams(
            dimension_semantics=("parallel",), vmem_limit_bytes=VMEM_LIMIT),
        name="s5_glu",
    )(y4, w_glu_t, gain_col)


def _out_proj_kernel(x_ref, ya_ref, ys_ref, yg_ref, w_ref, g_ref, o_ref, h_ref):
    y = jnp.concatenate([ya_ref[...], ys_ref[...], yg_ref[...]], axis=1)
    xn = x_ref[...] + jnp.dot(y, w_ref[...], preferred_element_type=F32)
    o_ref[...] = xn
    h_ref[...] = _rms_lanes(xn, g_ref[...]).astype(BF16)


def _out_proj(x2, ya, ys, yg, w_out_bf, gain_ffn, layer, tm):
    t = x2.shape[0]
    row = lambda i: (i, 0)
    const = lambda i: (0, 0)
    return pl.pallas_call(
        _out_proj_kernel,
        grid=(t // tm,),
        in_specs=[
            pl.BlockSpec((tm, D_MODEL), row),
            pl.BlockSpec((tm, ATTN_WIDTH), row),
            pl.BlockSpec((tm, SSM_WIDTH), row),
            pl.BlockSpec((tm, GMLP_WIDTH), row),
            pl.BlockSpec((None,) + w_out_bf.shape[1:], lambda i: (layer, 0, 0),
                         pipeline_mode=pl.Buffered(1)),
            pl.BlockSpec((1, D_MODEL), const),
        ],
        out_specs=[pl.BlockSpec((tm, D_MODEL), row), pl.BlockSpec((tm, D_MODEL), row)],
        out_shape=[jax.ShapeDtypeStruct((t, D_MODEL), F32), jax.ShapeDtypeStruct((t, D_MODEL), BF16)],
        compiler_params=pltpu.CompilerParams(
            dimension_semantics=("parallel",), vmem_limit_bytes=VMEM_LIMIT),
        name="out_proj",
    )(x2, ya, ys, yg, w_out_bf, gain_ffn)


FFN_RES_CHUNK = 256
FFN_RES_STEPS = D_MODEL // FFN_RES_CHUNK


def _ffn_kernel(h_ref, xc_ref, wg_ref, wu_ref, wd_ref, gf_ref, o_ref, *, final_norm, halves):
    j = pl.program_id(1)

    @pl.when(j == 0)
    def _():
        o_ref[...] = jnp.zeros_like(o_ref)

    @pl.when(j < FFN_RES_STEPS)
    def _():
        cols = pl.ds(pl.multiple_of(j * FFN_RES_CHUNK, FFN_RES_CHUNK), FFN_RES_CHUNK)
        o_ref[:, cols] += xc_ref[...]

    h = h_ref[...]
    hw = wg_ref.shape[1] // halves
    acc = None
    for c in range(halves):
        gate = jnp.dot(h, wg_ref[:, c * hw:(c + 1) * hw], preferred_element_type=F32)
        up = jnp.dot(h, wu_ref[:, c * hw:(c + 1) * hw], preferred_element_type=F32)
        a = (gate * jax.nn.sigmoid(gate) * up).astype(BF16)
        d = jnp.dot(a, wd_ref[c * hw:(c + 1) * hw, :], preferred_element_type=F32)
        acc = d if acc is None else acc + d
    o_ref[...] += acc

    if final_norm:
        @pl.when(j == pl.num_programs(1) - 1)
        def _():
            o_ref[...] = _rms_lanes(o_ref[...], gf_ref[...])


def _ffn(h, x2, wg_bf, wu_bf, wd_bf, gain_final, layer, tm, tf, final_norm):
    t = x2.shape[0]
    d_ff = wg_bf.shape[2]
    assert d_ff // tf >= FFN_RES_STEPS
    return pl.pallas_call(
        functools.partial(_ffn_kernel, final_norm=final_norm, halves=2),
        grid=(t // tm, d_ff // tf),
        in_specs=[
            pl.BlockSpec((tm, D_MODEL), lambda i, j: (i, 0)),
            pl.BlockSpec((tm, FFN_RES_CHUNK), lambda i, j: (i, jnp.minimum(j, FFN_RES_STEPS - 1))),
            pl.BlockSpec((None, D_MODEL, tf), lambda i, j: (layer, 0, j)),
            pl.BlockSpec((None, D_MODEL, tf), lambda i, j: (layer, 0, j)),
            pl.BlockSpec((None, tf, D_MODEL), lambda i, j: (layer, j, 0)),
            pl.BlockSpec((1, D_MODEL), lambda i, j: (0, 0)),
        ],
        out_specs=pl.BlockSpec((tm, D_MODEL), lambda i, j: (i, 0)),
        out_shape=jax.ShapeDtypeStruct((t, D_MODEL), F32),
        compiler_params=pltpu.CompilerParams(
            dimension_semantics=("parallel", "arbitrary"), vmem_limit_bytes=VMEM_LIMIT),
        name="ffn",
    )(h, x2, wg_bf, wu_bf, wd_bf, gain_final)


def kernel(x, norm_mix, w_in, attn_sinks, ssm_lam_re, ssm_lam_im, ssm_log_dt, ssm_b_re, ssm_b_im,
           ssm_c_re, ssm_c_im, ssm_d, ssm_w_glu, gmlp_ln_g, gmlp_ln_b, gmlp_w_s, gmlp_b_s,
           out_norm_attn, out_norm_ssm, out_norm_gmlp, w_out, norm_ffn, w_gate, w_up, w_down,
           norm_final):
    bsz, seq, _ = x.shape
    depth = w_in.shape[0]
    t = bsz * seq
    chunks_per_seq = seq // SSM_CHUNK
    assert seq % 1024 == 0 and (1 << SSM_SCAN_STEPS) == chunks_per_seq
    rows = t // SSM_CHUNK

    w_in_bf = _to_bf16(w_in, rows=256)
    w_out_bf = _to_bf16(w_out, rows=256)
    w_gate_bf = _to_bf16(w_gate, rows=256)
    w_up_bf = _to_bf16(w_up, rows=256)
    w_down_bf = _to_bf16(w_down, rows=512)
    ssm_tbls = _ssm_tables(ssm_lam_re, ssm_lam_im, ssm_log_dt, ssm_b_re, ssm_b_im,
                           ssm_c_re, ssm_c_im, ssm_d)

    causal = jnp.tril(jnp.ones((GMLP_CHUNK, GMLP_CHUNK), dtype=bool))
    head_of = ((jnp.arange(N_KV_HEADS)[None, :, None] * 4 + (jnp.arange(4 * ATTN_BLOCK) // ATTN_BLOCK)[:, None, None]) * 2
               + jnp.arange(2)[None, None, :]).reshape(4 * ATTN_BLOCK, 2 * N_KV_HEADS)

    x2 = x.reshape(t, D_MODEL)
    for l in range(depth):
        wst = _ssm_weight_t(w_in, l, cols=256)
        ws = jnp.where(causal[None], gmlp_w_s[l], 0.0).astype(BF16)
        q, kv, y_gmlp, u4 = _in_proj(x2, norm_mix[l][None], w_in_bf, wst, gmlp_ln_g[l][None],
                                     gmlp_ln_b[l][None], ws, gmlp_b_s[l].T, out_norm_gmlp[l][None],
                                     l, tm=1024)

        sink_cols = jnp.broadcast_to(
            (attn_sinks[l].astype(F32) * LOG2E)[head_of].T[:, :, None], (2 * N_KV_HEADS, 4 * ATTN_BLOCK, LANES))
        y_attn = _attention(q, kv, sink_cols, out_norm_attn[l][None], bsz, seq, nblk=8)

        y4 = _ssm(u4, *ssm_tbls, l, chunks_per_seq)
        y_ssm = _ssm_glu(y4, ssm_w_glu[l].T.astype(BF16), out_norm_ssm[l][:, None], tm=1024)

        x2, h = _out_proj(x2, y_attn, y_ssm, y_gmlp, w_out_bf, norm_ffn[l][None], l, tm=512)
        x2 = _ffn(h, x2, w_gate_bf, w_up_bf, w_down_bf, norm_final[None], l, tm=1024, tf=512,
                  final_norm=(l == depth - 1))
    return x2.reshape(bsz, seq, D_MODEL)
```

```python
import functools
import math

import jax
import jax.numpy as jnp
from jax import lax
from jax.experimental import pallas as pl
from jax.experimental.pallas import tpu as pltpu

F32 = jnp.float32
BF16 = jnp.bfloat16

D_MODEL = 2048
HEAD_DIM = 64
ATTN_WIDTH = 1024
N_Q_HEADS = ATTN_WIDTH // HEAD_DIM
N_KV_HEADS = 2
KV_WIDTH = N_KV_HEADS * HEAD_DIM
ATTN_BLOCK = 128
SSM_WIDTH = 512
SSM_GROUP = 16
SSM_GROUPS = SSM_WIDTH // SSM_GROUP
SSM_STATE = 64
GMLP_WIDTH = 512
GMLP_CHUNK = 128
GMLP_GROUPS = GMLP_WIDTH // 128
EPS = 1e-5
LOG2E = math.log2(math.e)
COL_KV = ATTN_WIDTH
COL_SSM = COL_KV + 2 * KV_WIDTH
COL_ZU = COL_SSM + SSM_WIDTH
COL_ZV = COL_ZU + GMLP_WIDTH
COL_END = COL_ZV + GMLP_WIDTH

LANES = 128
SUBLANES_F32 = 8
SSM_CHUNK = LANES
SSM_SCAN_STEPS = 7
GLU_SLABS = 2
VMEM_LIMIT = 56 * 1024 * 1024


def _gelu_tanh(x):
    k = -2.0 * math.sqrt(2.0 / math.pi) * LOG2E
    return x / (1.0 + jnp.exp2(x * ((x * x) * (0.044715 * k) + k)))


def _cast_kernel(x_ref, o_ref):
    o_ref[...] = x_ref[...].astype(o_ref.dtype)


def _to_bf16(w, rows):
    nl, r, c = w.shape
    return pl.pallas_call(
        _cast_kernel,
        grid=(nl, r // rows),
        in_specs=[pl.BlockSpec((None, rows, c), lambda l, i: (l, i, 0))],
        out_specs=pl.BlockSpec((None, rows, c), lambda l, i: (l, i, 0)),
        out_shape=jax.ShapeDtypeStruct(w.shape, BF16),
        compiler_params=pltpu.CompilerParams(
            dimension_semantics=("parallel", "parallel"), vmem_limit_bytes=VMEM_LIMIT),
        name="to_bf16",
    )(w)


def _rms_lanes(y, gain):
    ms = jnp.mean(y * y, axis=-1, keepdims=True)
    return y * lax.rsqrt(ms + EPS) * gain


def _transpose_cast_kernel(x_ref, o_ref):
    o_ref[...] = x_ref[...].T.astype(o_ref.dtype)


def _ssm_weight_t(w_in, layer, cols):
    assert COL_SSM % cols == 0 and SSM_WIDTH % cols == 0
    return pl.pallas_call(
        _transpose_cast_kernel,
        grid=(SSM_WIDTH // cols,),
        in_specs=[pl.BlockSpec((None, D_MODEL, cols), lambda i: (layer, 0, COL_SSM // cols + i))],
        out_specs=pl.BlockSpec((cols, D_MODEL), lambda i: (i, 0)),
        out_shape=jax.ShapeDtypeStruct((SSM_WIDTH, D_MODEL), BF16),
        compiler_params=pltpu.CompilerParams(
            dimension_semantics=("parallel",), vmem_limit_bytes=VMEM_LIMIT),
        name="ssm_weight_t",
    )(w_in)


def _gmlp_mix(zu, zv, ln_g, ln_b, ws_ref, bs, gain):
    tm = zu.shape[0]
    u = _gelu_tanh(zu)
    v = _gelu_tanh(zv)
    vc = v - jnp.mean(v, axis=-1, keepdims=True)
    v = vc * lax.rsqrt(jnp.mean(vc * vc, axis=-1, keepdims=True) + EPS) * ln_g + ln_b
    vb = v.astype(BF16)
    cols = []
    for g in range(GMLP_GROUPS):
        w = ws_ref[g]
        parts = []
        for c in range(tm // GMLP_CHUNK):
            vblk = vb[c * GMLP_CHUNK:(c + 1) * GMLP_CHUNK, g * LANES:(g + 1) * LANES]
            parts.append(jnp.dot(w, vblk, preferred_element_type=F32) + bs[:, g:g + 1])
        cols.append(jnp.concatenate(parts, axis=0))
    mixed = jnp.concatenate(cols, axis=1)
    return _rms_lanes(u * mixed, gain)


def _in_proj_kernel(x_ref, g_ref, w_ref, wst_ref, lng_ref, lnb_ref, ws_ref, bs_ref, gg_ref,
                    q_ref, kv_ref, yg_ref, ut_ref):
    h = _rms_lanes(x_ref[...], g_ref[...]).astype(BF16)

    def proj(lo, hi):
        return jnp.dot(h, w_ref[:, lo:hi], preferred_element_type=F32)

    zu = proj(COL_ZU, COL_ZV)
    zv = proj(COL_ZV, COL_END)
    parts = 4
    rm = x_ref.shape[0] // parts
    qw = ATTN_WIDTH // parts
    for p in range(parts):
        rows = slice(p * rm, (p + 1) * rm)
        q_ref[:, p * qw:(p + 1) * qw] = (proj(p * qw, (p + 1) * qw) * (HEAD_DIM ** -0.5 * LOG2E)).astype(BF16)
        yg_ref[rows, :] = _gmlp_mix(zu[rows], zv[rows], lng_ref[...], lnb_ref[...],
                                    ws_ref, bs_ref[...], gg_ref[...]).astype(BF16)
    kv_ref[...] = proj(COL_KV, COL_SSM).astype(BF16)
    ut = lax.dot_general(wst_ref[...], h, (((1,), (1,)), ((), ())), preferred_element_type=F32)
    for c in range(ut_ref.shape[2]):
        ut_ref[:, :, c, :] = ut[:, c * SSM_CHUNK:(c + 1) * SSM_CHUNK].reshape(
            SSM_GROUPS, SSM_GROUP, SSM_CHUNK)


def _in_proj(x2, gain, w_in_bf, wst, ln_g, ln_b, ws, bs_t, gain_gmlp, layer, tm):
    t = x2.shape[0]
    const = lambda i: (0, 0)
    row = lambda i: (i, 0)
    return pl.pallas_call(
        _in_proj_kernel,
        grid=(t // tm,),
        in_specs=[
            pl.BlockSpec((tm, D_MODEL), row),
            pl.BlockSpec((1, D_MODEL), const),
            pl.BlockSpec((None,) + w_in_bf.shape[1:], lambda i: (layer, 0, 0),
                         pipeline_mode=pl.Buffered(1)),
            pl.BlockSpec(wst.shape, const),
            pl.BlockSpec((1, GMLP_WIDTH), const),
            pl.BlockSpec((1, GMLP_WIDTH), const),
            pl.BlockSpec(ws.shape, lambda i: (0, 0, 0)),
            pl.BlockSpec(bs_t.shape, const),
            pl.BlockSpec((1, GMLP_WIDTH), const),
        ],
        out_specs=[
            pl.BlockSpec((tm, ATTN_WIDTH), row),
            pl.BlockSpec((tm, 2 * KV_WIDTH), row),
            pl.BlockSpec((tm, GMLP_WIDTH), row),
            pl.BlockSpec((SSM_GROUPS, SSM_GROUP, tm // SSM_CHUNK, SSM_CHUNK), lambda i: (0, 0, i, 0)),
        ],
        out_shape=[
            jax.ShapeDtypeStruct((t, ATTN_WIDTH), BF16),
            jax.ShapeDtypeStruct((t, 2 * KV_WIDTH), BF16),
            jax.ShapeDtypeStruct((t, GMLP_WIDTH), BF16),
            jax.ShapeDtypeStruct((SSM_GROUPS, SSM_GROUP, t // SSM_CHUNK, SSM_CHUNK), F32),
        ],
        compiler_params=pltpu.CompilerParams(
            dimension_semantics=("parallel",), vmem_limit_bytes=VMEM_LIMIT),
        name="in_proj",
    )(x2, gain, w_in_bf, wst, ln_g, ln_b, ws, bs_t, gain_gmlp)


def _attn_kernel(q_ref, kvp_ref, kvc_ref, sink_ref, g_ref, o_ref, *, nblk):
    first_tile = pl.program_id(1) == 0
    lo = lax.broadcasted_iota(jnp.int32, (ATTN_BLOCK, LANES), 1) < HEAD_DIM

    def padded(kvblk):
        kvf = kvblk.astype(F32)
        kk, vv = kvf[:, :LANES], kvf[:, LANES:]
        kk_sw = pltpu.roll(kk, HEAD_DIM, axis=1)
        vv_sw = pltpu.roll(vv, HEAD_DIM, axis=1)
        ke = [jnp.where(lo, kk, 0.0), jnp.where(lo, kk_sw, 0.0)]
        ko = [jnp.where(lo, 0.0, kk_sw), jnp.where(lo, 0.0, kk)]
        ve = [jnp.where(lo, vv, 0.0), jnp.where(lo, vv_sw, 0.0)]
        vo = [jnp.where(lo, 0.0, vv_sw), jnp.where(lo, 0.0, vv)]
        return [[a.astype(BF16) for a in lst] for lst in (ke, ko, ve, vo)]

    keys = [padded(kvp_ref[...])] + [
        padded(kvc_ref[i * ATTN_BLOCK:(i + 1) * ATTN_BLOCK, :]) for i in range(nblk)]

    rows = 4 * ATTN_BLOCK
    qi = lax.broadcasted_iota(jnp.int32, (rows, LANES), 0) & (ATTN_BLOCK - 1)
    ci = lax.broadcasted_iota(jnp.int32, (rows, LANES), 1)
    vis_prev = ci > qi
    vis_prev_first = vis_prev & jnp.logical_not(first_tile)
    lo_rows = ci < HEAD_DIM
    even_key_rows = lax.broadcasted_iota(jnp.int32, (rows, LANES), 0) < 2 * ATTN_BLOCK
    ones = jnp.where(lo_rows == even_key_rows, 1.0, 0.0).astype(BF16)

    for i in range(nblk):
        q = q_ref[i * ATTN_BLOCK:(i + 1) * ATTN_BLOCK, :]
        vp = vis_prev_first if i == 0 else vis_prev
        outs = []
        for kh in range(N_KV_HEADS):
            kz = jnp.concatenate([keys[i][0][kh], keys[i + 1][0][kh],
                                  keys[i][1][kh], keys[i + 1][1][kh]], axis=0)
            vz = jnp.concatenate([keys[i][2][kh], keys[i + 1][2][kh],
                                  keys[i][3][kh], keys[i + 1][3][kh]], axis=0)
            qs = jnp.concatenate(
                [q[:, (kh * 4 + p) * LANES:(kh * 4 + p + 1) * LANES] for p in range(4)], axis=0)
            s = lax.dot_general(qs, kz, (((1,), (1,)), ((), ())), preferred_element_type=F32)
            es, ts = [], []
            for par in range(2):
                c0 = par * 2 * ATTN_BLOCK
                sp = jnp.where(vp, s[:, c0:c0 + ATTN_BLOCK], -jnp.inf)
                sc = jnp.where(vis_prev, -jnp.inf, s[:, c0 + ATTN_BLOCK:c0 + 2 * ATTN_BLOCK])
                sk = sink_ref[kh * 2 + par]
                m = jnp.maximum(jnp.max(jnp.maximum(sp, sc), axis=-1, keepdims=True), sk)
                es.append(jnp.exp2(sp - m).astype(BF16))
                es.append(jnp.exp2(sc - m).astype(BF16))
                ts.append(jnp.exp2(sk - m))
            e = jnp.concatenate(es, axis=1)
            od = jnp.dot(e, jnp.concatenate([vz, ones], axis=1), preferred_element_type=F32)
            den = od[:, LANES:] + jnp.where(lo_rows, ts[0], ts[1])
            o = od[:, :LANES] / den
            outs.extend([o[pp * ATTN_BLOCK:(pp + 1) * ATTN_BLOCK] for pp in range(4)])
        y = jnp.concatenate(outs, axis=1)
        o_ref[i * ATTN_BLOCK:(i + 1) * ATTN_BLOCK, :] = _rms_lanes(y, g_ref[...]).astype(BF16)


def _attention(q, kv, sink_cols, gain, bsz, seq, nblk):
    t = q.shape[0]
    tq = nblk * ATTN_BLOCK
    nt = seq // tq
    return pl.pallas_call(
        functools.partial(_attn_kernel, nblk=nblk),
        grid=(bsz, nt),
        in_specs=[
            pl.BlockSpec((tq, ATTN_WIDTH), lambda b, n: (b * nt + n, 0)),
            pl.BlockSpec((ATTN_BLOCK, 2 * KV_WIDTH),
                         lambda b, n: ((b * nt + n) * nblk - jnp.minimum(n, 1), 0)),
            pl.BlockSpec((tq, 2 * KV_WIDTH), lambda b, n: (b * nt + n, 0)),
            pl.BlockSpec(sink_cols.shape, lambda b, n: (0, 0, 0)),
            pl.BlockSpec((1, ATTN_WIDTH), lambda b, n: (0, 0)),
        ],
        out_specs=pl.BlockSpec((tq, ATTN_WIDTH), lambda b, n: (b * nt + n, 0)),
        out_shape=jax.ShapeDtypeStruct((t, ATTN_WIDTH), BF16),
        compiler_params=pltpu.CompilerParams(
            dimension_semantics=("parallel", "parallel"), vmem_limit_bytes=VMEM_LIMIT),
        name="swa_attention",
    )(q, kv, kv, sink_cols, gain)


def _ssm_kernel(u_ref, kfirst_ref, kodd_ref, knext_ref, pw_ref, bc_ref, cc_ref, ap_ref, y_ref,
                m_even, m_odd, *, chunks_per_seq):
    rows = u_ref.shape[2]
    tri = (lax.broadcasted_iota(jnp.int32, (LANES, LANES), 1)
           >= lax.broadcasted_iota(jnp.int32, (LANES, LANES), 0))

    def build(k_ref, m_ref):
        for hp in range(SSM_GROUP):
            for h in range(SSM_GROUP):
                k = k_ref[0, hp * SSM_GROUP + h:hp * SSM_GROUP + h + 1, :]
                kb = jnp.broadcast_to(k, (LANES, LANES))
                kr = pltpu.roll(kb, 0, 1, stride=1, stride_axis=0)
                m_ref[hp * LANES:(hp + 1) * LANES, h * LANES:(h + 1) * LANES] = (
                    jnp.where(tri, kr, 0.0).astype(BF16))

    cidx = lax.broadcasted_iota(jnp.int32, (rows, LANES), 0) & (chunks_per_seq - 1)

    def mix(slot, m_ref):
        u = jnp.concatenate([u_ref[slot, hp].astype(BF16) for hp in range(SSM_GROUP)], axis=1)
        y = jnp.dot(u, m_ref[...], preferred_element_type=F32)
        e = jnp.concatenate(
            [(pw_ref[slot, 0] * bc_ref[slot, 0, hp:hp + 1, :]
              + pw_ref[slot, 1] * bc_ref[slot, 1, hp:hp + 1, :]).astype(BF16)
             for hp in range(SSM_GROUP)], axis=0)
        s = jnp.dot(u, e, preferred_element_type=F32)
        for j in range(SSM_SCAN_STEPS):
            d = 1 << j
            prev = jnp.where(cidx >= d, pltpu.roll(s, d, axis=0), 0.0)
            a_same = ap_ref[slot, 2 * j:2 * j + 1, :]
            a_cross = ap_ref[slot, 2 * j + 1:2 * j + 2, :]
            s = s + prev * a_same + pltpu.roll(prev, SSM_STATE, axis=1) * a_cross
        s_prev = jnp.where(cidx >= 1, pltpu.roll(s, 1, axis=0), 0.0)
        f = jnp.concatenate(
            [(cc_ref[slot, :, h:h + 1] * pw_ref[slot, 2]
              + cc_ref[slot, :, SSM_GROUP + h:SSM_GROUP + h + 1] * pw_ref[slot, 3]).astype(BF16)
             for h in range(SSM_GROUP)], axis=1)
        y = y + jnp.dot(s_prev.astype(BF16), f, preferred_element_type=F32)
        for h in range(SSM_GROUP):
            y_ref[slot, h] = y[:, h * LANES:(h + 1) * LANES]

    @pl.when(pl.program_id(0) == 0)
    def _():
        build(kfirst_ref, m_even)

    mix(0, m_even)
    build(kodd_ref, m_odd)
    mix(1, m_odd)
    build(knext_ref, m_even)


def _ssm(u4, kc, pw, bc, cc, ap, layer, chunks_per_seq):
    g, hh, rows, _ = u4.shape
    kblk = (None, 1) + kc.shape[2:]
    return pl.pallas_call(
        functools.partial(_ssm_kernel, chunks_per_seq=chunks_per_seq),
        grid=(g // 2,),
        in_specs=[
            pl.BlockSpec((2, hh, rows, LANES), lambda i: (i, 0, 0, 0)),
            pl.BlockSpec(kblk, lambda i: (layer, 0, 0, 0)),
            pl.BlockSpec(kblk, lambda i: (layer, 2 * i + 1, 0, 0)),
            pl.BlockSpec(kblk, lambda i: (layer, jnp.minimum(2 * i + 2, g - 1), 0, 0)),
            pl.BlockSpec((None, 2) + pw.shape[2:], lambda i: (layer, i, 0, 0, 0)),
            pl.BlockSpec((None, 2) + bc.shape[2:], lambda i: (layer, i, 0, 0, 0)),
            pl.BlockSpec((None, 2) + cc.shape[2:], lambda i: (layer, i, 0, 0)),
            pl.BlockSpec((None, 2) + ap.shape[2:], lambda i: (layer, i, 0, 0)),
        ],
        out_specs=pl.BlockSpec((2, hh, rows, LANES), lambda i: (i, 0, 0, 0)),
        out_shape=jax.ShapeDtypeStruct(u4.shape, F32),
        scratch_shapes=[pltpu.VMEM((SSM_GROUP * LANES, SSM_GROUP * LANES), BF16),
                        pltpu.VMEM((SSM_GROUP * LANES, SSM_GROUP * LANES), BF16)],
        compiler_params=pltpu.CompilerParams(
            dimension_semantics=("arbitrary",), vmem_limit_bytes=VMEM_LIMIT),
        name="s5_chunked",
    )(u4, kc, kc, kc, pw, bc, cc, ap)


def _ssm_tables(lam_re, lam_im, log_dt, b_re, b_im, c_re, c_im, d_skip):
    hp = lax.Precision.HIGHEST
    nl = lam_re.shape[0]
    lr, li = lam_re.astype(F32), lam_im.astype(F32)
    dt = jnp.exp(log_dt.astype(F32))[..., None]
    mag = jnp.exp(lr * dt)
    ar, ai = mag * jnp.cos(li * dt), mag * jnp.sin(li * dt)
    den = lr * lr + li * li
    nre, nim = ar - 1.0, ai
    coef_re = (nre * lr + nim * li) / den
    coef_im = (nim * lr - nre * li) / den
    br, bi = b_re.astype(F32), b_im.astype(F32)
    bbr = coef_re[..., None] * br - coef_im[..., None] * bi
    bbi = coef_re[..., None] * bi + coef_im[..., None] * br
    crp = jnp.swapaxes(c_re.astype(F32), -1, -2)
    cip = jnp.swapaxes(c_im.astype(F32), -1, -2)
    n = jnp.arange(SSM_CHUNK + 1, dtype=F32)
    pmag = jnp.exp((lr * dt)[..., None] * n)
    pang = (li * dt)[..., None] * n
    pr, pi = pmag * jnp.cos(pang), pmag * jnp.sin(pang)
    wr = bbr[..., :, None] * crp[..., None, :] - bbi[..., :, None] * cip[..., None, :]
    wi = bbr[..., :, None] * cip[..., None, :] + bbi[..., :, None] * crp[..., None, :]
    taps = (jnp.einsum("lgpn,lgpab->lgabn", pr[..., :SSM_CHUNK], wr, precision=hp)
            - jnp.einsum("lgpn,lgpab->lgabn", pi[..., :SSM_CHUNK], wi, precision=hp))
    dskip = d_skip.astype(F32).reshape(nl, SSM_GROUPS, SSM_GROUP)
    lag0 = (jnp.arange(SSM_CHUNK) == 0).astype(F32)
    taps = taps + (jnp.eye(SSM_GROUP, dtype=F32) * dskip[:, :, None, :])[..., None] * lag0
    kc = taps.reshape(nl, SSM_GROUPS, SSM_GROUP * SSM_GROUP, SSM_CHUNK)
    prr = jnp.swapaxes(pr[..., :SSM_CHUNK][..., ::-1], -1, -2)
    pir = jnp.swapaxes(pi[..., :SSM_CHUNK][..., ::-1], -1, -2)
    pw = jnp.stack([jnp.concatenate([prr, prr], axis=-1),
                    jnp.concatenate([pir, pir], axis=-1),
                    jnp.concatenate([pr[..., 1:], pr[..., 1:]], axis=2),
                    jnp.concatenate([pi[..., 1:], pi[..., 1:]], axis=2)], axis=2)
    bbr_t = jnp.swapaxes(bbr, -1, -2)
    bbi_t = jnp.swapaxes(bbi, -1, -2)
    bc = jnp.stack([jnp.concatenate([bbr_t, bbi_t], axis=-1),
                    jnp.concatenate([-bbi_t, bbr_t], axis=-1)], axis=2)
    cc = jnp.concatenate([jnp.concatenate([crp, -cip], axis=2),
                          jnp.concatenate([-cip, -crp], axis=2)], axis=-1)
    a_r, a_i = pr[..., SSM_CHUNK], pi[..., SSM_CHUNK]
    rows = []
    for _ in range(SSM_SCAN_STEPS):
        rows.append(jnp.concatenate([a_r, a_r], axis=-1))
        rows.append(jnp.concatenate([-a_i, a_i], axis=-1))
        a_r, a_i = a_r * a_r - a_i * a_i, 2.0 * a_r * a_i
    rows.extend([jnp.zeros_like(rows[0])] * (16 - len(rows)))
    ap = jnp.stack(rows, axis=2)
    return kc, pw, bc, cc, ap


def _glu_kernel(y4_ref, wt_ref, g_ref, o_ref):
    def slab(c):
        return y4_ref[:, :, c, :].reshape(SSM_WIDTH, SSM_CHUNK)

    for c in range(0, y4_ref.shape[2], GLU_SLABS):
        y = _gelu_tanh(jnp.concatenate([slab(c + i) for i in range(GLU_SLABS)], axis=1))
        z = jnp.dot(wt_ref[...], y.astype(BF16), preferred_element_type=F32)
        o = y * jax.nn.sigmoid(z)
        ms = jnp.mean(o * o, axis=0, keepdims=True)
        o = o * lax.rsqrt(ms + EPS) * g_ref[...]
        o_ref[c * SSM_CHUNK:(c + GLU_SLABS) * SSM_CHUNK, :] = o.T.astype(BF16)


def _ssm_glu(y4, w_glu_t, gain_col, tm):
    t = y4.shape[2] * SSM_CHUNK
    return pl.pallas_call(
        _glu_kernel,
        grid=(t // tm,),
        in_specs=[
            pl.BlockSpec((SSM_GROUPS, SSM_GROUP, tm // SSM_CHUNK, SSM_CHUNK), lambda i: (0, 0, i, 0)),
            pl.BlockSpec((SSM_WIDTH, SSM_WIDTH), lambda i: (0, 0)),
            pl.BlockSpec((SSM_WIDTH, 1), lambda i: (0, 0)),
        ],
        out_specs=pl.BlockSpec((tm, SSM_WIDTH), lambda i: (i, 0)),
        out_shape=jax.ShapeDtypeStruct((t, SSM_WIDTH), BF16),
        compiler_params=pltpu.CompilerParams(
            dimension_semantics=("parallel",), vmem_limit_bytes=VMEM_LIMIT),
        name="s5_glu",
    )(y4, w_glu_t, gain_col)


def _out_proj_kernel(x_ref, ya_ref, ys_ref, yg_ref, w_ref, g_ref, o_ref, h_ref):
    y = jnp.concatenate([ya_ref[...], ys_ref[...], yg_ref[...]], axis=1)
    xn = x_ref[...] + jnp.dot(y, w_ref[...], preferred_element_type=F32)
    o_ref[...] = xn
    h_ref[...] = _rms_lanes(xn, g_ref[...]).astype(BF16)


def _out_proj(x2, ya, ys, yg, w_out_bf, gain_ffn, layer, tm):
    t = x2.shape[0]
    row = lambda i: (i, 0)
    const = lambda i: (0, 0)
    return pl.pallas_call(
        _out_proj_kernel,
        grid=(t // tm,),
        in_specs=[
            pl.BlockSpec((tm, D_MODEL), row),
            pl.BlockSpec((tm, ATTN_WIDTH), row),
            pl.BlockSpec((tm, SSM_WIDTH), row),
            pl.BlockSpec((tm, GMLP_WIDTH), row),
            pl.BlockSpec((None,) + w_out_bf.shape[1:], lambda i: (layer, 0, 0),
                         pipeline_mode=pl.Buffered(1)),
            pl.BlockSpec((1, D_MODEL), const),
        ],
        out_specs=[pl.BlockSpec((tm, D_MODEL), row), pl.BlockSpec((tm, D_MODEL), row)],
        out_shape=[jax.ShapeDtypeStruct((t, D_MODEL), F32), jax.ShapeDtypeStruct((t, D_MODEL), BF16)],
        compiler_params=pltpu.CompilerParams(
            dimension_semantics=("parallel",), vmem_limit_bytes=VMEM_LIMIT),
        name="out_proj",
    )(x2, ya, ys, yg, w_out_bf, gain_ffn)


FFN_RES_CHUNK = 256
FFN_RES_STEPS = D_MODEL // FFN_RES_CHUNK


def _ffn_kernel(h_ref, xc_ref, wg_ref, wu_ref, wd_ref, gf_ref, o_ref, *, final_norm, halves):
    j = pl.program_id(1)

    @pl.when(j == 0)
    def _():
        o_ref[...] = jnp.zeros_like(o_ref)

    @pl.when(j < FFN_RES_STEPS)
    def _():
        cols = pl.ds(pl.multiple_of(j * FFN_RES_CHUNK, FFN_RES_CHUNK), FFN_RES_CHUNK)
        o_ref[:, cols] += xc_ref[...]

    h = h_ref[...]
    hw = wg_ref.shape[1] // halves
    acc = None
    for c in range(halves):
        gate = jnp.dot(h, wg_ref[:, c * hw:(c + 1) * hw], preferred_element_type=F32)
        up = jnp.dot(h, wu_ref[:, c * hw:(c + 1) * hw], preferred_element_type=F32)
        a = (gate * jax.nn.sigmoid(gate) * up).astype(BF16)
        d = jnp.dot(a, wd_ref[c * hw:(c + 1) * hw, :], preferred_element_type=F32)
        acc = d if acc is None else acc + d
    o_ref[...] += acc

    if final_norm:
        @pl.when(j == pl.num_programs(1) - 1)
        def _():
            o_ref[...] = _rms_lanes(o_ref[...], gf_ref[...])


def _ffn(h, x2, wg_bf, wu_bf, wd_bf, gain_final, layer, tm, tf, final_norm):
    t = x2.shape[0]
    d_ff = wg_bf.shape[2]
    assert d_ff // tf >= FFN_RES_STEPS
    return pl.pallas_call(
        functools.partial(_ffn_kernel, final_norm=final_norm, halves=2),
        grid=(t // tm, d_ff // tf),
        in_specs=[
            pl.BlockSpec((tm, D_MODEL), lambda i, j: (i, 0)),
            pl.BlockSpec((tm, FFN_RES_CHUNK), lambda i, j: (i, jnp.minimum(j, FFN_RES_STEPS - 1))),
            pl.BlockSpec((None, D_MODEL, tf), lambda i, j: (layer, 0, j)),
            pl.BlockSpec((None, D_MODEL, tf), lambda i, j: (layer, 0, j)),
            pl.BlockSpec((None, tf, D_MODEL), lambda i, j: (layer, j, 0)),
            pl.BlockSpec((1, D_MODEL), lambda i, j: (0, 0)),
        ],
        out_specs=pl.BlockSpec((tm, D_MODEL), lambda i, j: (i, 0)),
        out_shape=jax.ShapeDtypeStruct((t, D_MODEL), F32),
        compiler_params=pltpu.CompilerParams(
            dimension_semantics=("parallel", "arbitrary"), vmem_limit_bytes=VMEM_LIMIT),
        name="ffn",
    )(h, x2, wg_bf, wu_bf, wd_bf, gain_final)


def kernel(x, norm_mix, w_in, attn_sinks, ssm_lam_re, ssm_lam_im, ssm_log_dt, ssm_b_re, ssm_b_im,
           ssm_c_re, ssm_c_im, ssm_d, ssm_w_glu, gmlp_ln_g, gmlp_ln_b, gmlp_w_s, gmlp_b_s,
           out_norm_attn, out_norm_ssm, out_norm_gmlp, w_out, norm_ffn, w_gate, w_up, w_down,
           norm_final):
    bsz, seq, _ = x.shape
    depth = w_in.shape[0]
    t = bsz * seq
    chunks_per_seq = seq // SSM_CHUNK
    assert seq % 1024 == 0 and (1 << SSM_SCAN_STEPS) == chunks_per_seq
    rows = t // SSM_CHUNK

    w_in_bf = _to_bf16(w_in, rows=256)
    w_out_bf = _to_bf16(w_out, rows=256)
    w_gate_bf = _to_bf16(w_gate, rows=256)
    w_up_bf = _to_bf16(w_up, rows=256)
    w_down_bf = _to_bf16(w_down, rows=512)
    ssm_tbls = _ssm_tables(ssm_lam_re, ssm_lam_im, ssm_log_dt, ssm_b_re, ssm_b_im,
                           ssm_c_re, ssm_c_im, ssm_d)

    causal = jnp.tril(jnp.ones((GMLP_CHUNK, GMLP_CHUNK), dtype=bool))
    head_of = ((jnp.arange(N_KV_HEADS)[None, :, None] * 4 + (jnp.arange(4 * ATTN_BLOCK) // ATTN_BLOCK)[:, None, None]) * 2
               + jnp.arange(2)[None, None, :]).reshape(4 * ATTN_BLOCK, 2 * N_KV_HEADS)

    x2 = x.reshape(t, D_MODEL)
    for l in range(depth):
        wst = _ssm_weight_t(w_in, l, cols=256)
        ws = jnp.where(causal[None], gmlp_w_s[l], 0.0).astype(BF16)
        q, kv, y_gmlp, u4 = _in_proj(x2, norm_mix[l][None], w_in_bf, wst, gmlp_ln_g[l][None],
                                     gmlp_ln_b[l][None], ws, gmlp_b_s[l].T, out_norm_gmlp[l][None],
                                     l, tm=1024)

        sink_cols = jnp.broadcast_to(
            (attn_sinks[l].astype(F32) * LOG2E)[head_of].T[:, :, None], (2 * N_KV_HEADS, 4 * ATTN_BLOCK, LANES))
        y_attn = _attention(q, kv, sink_cols, out_norm_attn[l][None], bsz, seq, nblk=16)

        y4 = _ssm(u4, *ssm_tbls, l, chunks_per_seq)
        y_ssm = _ssm_glu(y4, ssm_w_glu[l].T.astype(BF16), out_norm_ssm[l][:, None], tm=2048)

        x2, h = _out_proj(x2, y_attn, y_ssm, y_gmlp, w_out_bf, norm_ffn[l][None], l, tm=512)
        x2 = _ffn(h, x2, w_gate_bf, w_up_bf, w_down_bf, norm_final[None], l, tm=1024, tf=512,
                  final_norm=(l == depth - 1))
    return x2.reshape(bsz, seq, D_MODEL)
```

```python
import functools
import math

import jax
import jax.numpy as jnp
from jax import lax
from jax.experimental import pallas as pl
from jax.experimental.pallas import tpu as pltpu

F32 = jnp.float32
BF16 = jnp.bfloat16

D_MODEL = 2048
HEAD_DIM = 64
ATTN_WIDTH = 1024
N_KV_HEADS = 2
KV_WIDTH = N_KV_HEADS * HEAD_DIM
ATTN_BLOCK = 128
SSM_WIDTH = 512
SSM_GROUP = 16
SSM_GROUPS = SSM_WIDTH // SSM_GROUP
SSM_STATE = 64
GMLP_WIDTH = 512
GMLP_CHUNK = 128
GMLP_GROUPS = GMLP_WIDTH // 128
EPS = 1e-5
LOG2E = math.log2(math.e)
COL_KV = ATTN_WIDTH
COL_SSM = COL_KV + 2 * KV_WIDTH
COL_ZU = COL_SSM + SSM_WIDTH
COL_ZV = COL_ZU + GMLP_WIDTH
COL_END = COL_ZV + GMLP_WIDTH

LANES = 128
SSM_CHUNK = LANES
SSM_SCAN_STEPS = 7
GLU_SLABS = 2
VMEM_LIMIT = 56 * 1024 * 1024


def _gelu_tanh(x):
    k = -2.0 * math.sqrt(2.0 / math.pi) * LOG2E
    return x / (1.0 + jnp.exp2(x * ((x * x) * (0.044715 * k) + k)))


def _cast_kernel(x_ref, o_ref):
    o_ref[...] = x_ref[...].astype(o_ref.dtype)


def _to_bf16(w, rows):
    nl, r, c = w.shape
    return pl.pallas_call(
        _cast_kernel,
        grid=(nl, r // rows),
        in_specs=[pl.BlockSpec((None, rows, c), lambda l, i: (l, i, 0))],
        out_specs=pl.BlockSpec((None, rows, c), lambda l, i: (l, i, 0)),
        out_shape=jax.ShapeDtypeStruct(w.shape, BF16),
        compiler_params=pltpu.CompilerParams(
            dimension_semantics=("parallel", "parallel"), vmem_limit_bytes=VMEM_LIMIT),
        name="to_bf16",
    )(w)


def _rms_lanes(y, gain):
    ms = jnp.mean(y * y, axis=-1, keepdims=True)
    return y * lax.rsqrt(ms + EPS) * gain


def _transpose_cast_kernel(x_ref, o_ref):
    o_ref[...] = x_ref[...].T.astype(o_ref.dtype)


def _ssm_weight_t(w_in, layer, cols):
    assert COL_SSM % cols == 0 and SSM_WIDTH % cols == 0
    return pl.pallas_call(
        _transpose_cast_kernel,
        grid=(SSM_WIDTH // cols,),
        in_specs=[pl.BlockSpec((None, D_MODEL, cols), lambda i: (layer, 0, COL_SSM // cols + i))],
        out_specs=pl.BlockSpec((cols, D_MODEL), lambda i: (i, 0)),
        out_shape=jax.ShapeDtypeStruct((SSM_WIDTH, D_MODEL), BF16),
        compiler_params=pltpu.CompilerParams(
            dimension_semantics=("parallel",), vmem_limit_bytes=VMEM_LIMIT),
        name="ssm_weight_t",
    )(w_in)


def _gmlp_mix(zu, zv, ln_g, ln_b, ws_ref, bs, gain):
    tm = zu.shape[0]
    u = _gelu_tanh(zu)
    v = _gelu_tanh(zv)
    vc = v - jnp.mean(v, axis=-1, keepdims=True)
    v = vc * lax.rsqrt(jnp.mean(vc * vc, axis=-1, keepdims=True) + EPS) * ln_g + ln_b
    vb = v.astype(BF16)
    cols = []
    for g in range(GMLP_GROUPS):
        w = ws_ref[g]
        parts = []
        for c in range(tm // GMLP_CHUNK):
            vblk = vb[c * GMLP_CHUNK:(c + 1) * GMLP_CHUNK, g * LANES:(g + 1) * LANES]
            parts.append(jnp.dot(w, vblk, preferred_element_type=F32) + bs[:, g:g + 1])
        cols.append(jnp.concatenate(parts, axis=0))
    mixed = jnp.concatenate(cols, axis=1)
    return _rms_lanes(u * mixed, gain)


def _in_proj_kernel(x_ref, g_ref, w_ref, wst_ref, lng_ref, lnb_ref, ws_ref, bs_ref, gg_ref,
                    q_ref, kv_ref, yg_ref, ut_ref):
    h = _rms_lanes(x_ref[...], g_ref[...]).astype(BF16)

    def proj(lo, hi):
        return jnp.dot(h, w_ref[:, lo:hi], preferred_element_type=F32)

    zu = proj(COL_ZU, COL_ZV)
    zv = proj(COL_ZV, COL_END)
    parts = 4
    rm = x_ref.shape[0] // parts
    qw = ATTN_WIDTH // parts
    for p in range(parts):
        rows = slice(p * rm, (p + 1) * rm)
        q_ref[:, p * qw:(p + 1) * qw] = (proj(p * qw, (p + 1) * qw) * (HEAD_DIM ** -0.5 * LOG2E)).astype(BF16)
        yg_ref[rows, :] = _gmlp_mix(zu[rows], zv[rows], lng_ref[...], lnb_ref[...],
                                    ws_ref, bs_ref[...], gg_ref[...]).astype(BF16)
    kv_ref[...] = proj(COL_KV, COL_SSM).astype(BF16)
    ut = lax.dot_general(wst_ref[...], h, (((1,), (1,)), ((), ())), preferred_element_type=F32)
    for c in range(ut_ref.shape[2]):
        ut_ref[:, :, c, :] = ut[:, c * SSM_CHUNK:(c + 1) * SSM_CHUNK].reshape(
            SSM_GROUPS, SSM_GROUP, SSM_CHUNK)


def _in_proj(x2, gain, w_in_bf, wst, ln_g, ln_b, ws, bs_t, gain_gmlp, layer, tm):
    t = x2.shape[0]
    const = lambda i: (0, 0)
    row = lambda i: (i, 0)
    return pl.pallas_call(
        _in_proj_kernel,
        grid=(t // tm,),
        in_specs=[
            pl.BlockSpec((tm, D_MODEL), row),
            pl.BlockSpec((1, D_MODEL), const),
            pl.BlockSpec((None,) + w_in_bf.shape[1:], lambda i: (layer, 0, 0),
                         pipeline_mode=pl.Buffered(1)),
            pl.BlockSpec(wst.shape, const),
            pl.BlockSpec((1, GMLP_WIDTH), const),
            pl.BlockSpec((1, GMLP_WIDTH), const),
            pl.BlockSpec(ws.shape, lambda i: (0, 0, 0)),
            pl.BlockSpec(bs_t.shape, const),
            pl.BlockSpec((1, GMLP_WIDTH), const),
        ],
        out_specs=[
            pl.BlockSpec((tm, ATTN_WIDTH), row),
            pl.BlockSpec((tm, 2 * KV_WIDTH), row),
            pl.BlockSpec((tm, GMLP_WIDTH), row),
            pl.BlockSpec((SSM_GROUPS, SSM_GROUP, tm // SSM_CHUNK, SSM_CHUNK), lambda i: (0, 0, i, 0)),
        ],
        out_shape=[
            jax.ShapeDtypeStruct((t, ATTN_WIDTH), BF16),
            jax.ShapeDtypeStruct((t, 2 * KV_WIDTH), BF16),
            jax.ShapeDtypeStruct((t, GMLP_WIDTH), BF16),
            jax.ShapeDtypeStruct((SSM_GROUPS, SSM_GROUP, t // SSM_CHUNK, SSM_CHUNK), F32),
        ],
        compiler_params=pltpu.CompilerParams(
            dimension_semantics=("parallel",), vmem_limit_bytes=VMEM_LIMIT),
        name="in_proj",
    )(x2, gain, w_in_bf, wst, ln_g, ln_b, ws, bs_t, gain_gmlp)


def _attn_kernel(q_ref, kvp_ref, kvc_ref, sink_ref, g_ref, o_ref, *, nblk):
    first_tile = pl.program_id(1) == 0
    lo = lax.broadcasted_iota(jnp.int32, (ATTN_BLOCK, LANES), 1) < HEAD_DIM

    def padded(kvblk):
        kvf = kvblk.astype(F32)
        kk, vv = kvf[:, :LANES], kvf[:, LANES:]
        kk_sw = pltpu.roll(kk, HEAD_DIM, axis=1)
        vv_sw = pltpu.roll(vv, HEAD_DIM, axis=1)
        ke = [jnp.where(lo, kk, 0.0), jnp.where(lo, kk_sw, 0.0)]
        ko = [jnp.where(lo, 0.0, kk_sw), jnp.where(lo, 0.0, kk)]
        ve = [jnp.where(lo, vv, 0.0), jnp.where(lo, vv_sw, 0.0)]
        vo = [jnp.where(lo, 0.0, vv_sw), jnp.where(lo, 0.0, vv)]
        return [[a.astype(BF16) for a in lst] for lst in (ke, ko, ve, vo)]

    keys = [padded(kvp_ref[...])] + [
        padded(kvc_ref[i * ATTN_BLOCK:(i + 1) * ATTN_BLOCK, :]) for i in range(nblk)]

    rows = 4 * ATTN_BLOCK
    qi = lax.broadcasted_iota(jnp.int32, (rows, LANES), 0) & (ATTN_BLOCK - 1)
    ci = lax.broadcasted_iota(jnp.int32, (rows, LANES), 1)
    vis_prev = ci > qi
    vis_prev_first = vis_prev & jnp.logical_not(first_tile)
    lo_rows = ci < HEAD_DIM
    even_key_rows = lax.broadcasted_iota(jnp.int32, (rows, LANES), 0) < 2 * ATTN_BLOCK
    ones = jnp.where(lo_rows == even_key_rows, 1.0, 0.0).astype(BF16)

    for i in range(nblk):
        q = q_ref[i * ATTN_BLOCK:(i + 1) * ATTN_BLOCK, :]
        vp = vis_prev_first if i == 0 else vis_prev
        outs = []
        for kh in range(N_KV_HEADS):
            kz = jnp.concatenate([keys[i][0][kh], keys[i + 1][0][kh],
                                  keys[i][1][kh], keys[i + 1][1][kh]], axis=0)
            vz = jnp.concatenate([keys[i][2][kh], keys[i + 1][2][kh],
                                  keys[i][3][kh], keys[i + 1][3][kh]], axis=0)
            qs = jnp.concatenate(
                [q[:, (kh * 4 + p) * LANES:(kh * 4 + p + 1) * LANES] for p in range(4)], axis=0)
            s = lax.dot_general(qs, kz, (((1,), (1,)), ((), ())), preferred_element_type=F32)
            es, ts = [], []
            for par in range(2):
                c0 = par * 2 * ATTN_BLOCK
                sp = jnp.where(vp, s[:, c0:c0 + ATTN_BLOCK], -jnp.inf)
                sc = jnp.where(vis_prev, -jnp.inf, s[:, c0 + ATTN_BLOCK:c0 + 2 * ATTN_BLOCK])
                sk = sink_ref[kh * 2 + par]
                m = jnp.maximum(jnp.max(jnp.maximum(sp, sc), axis=-1, keepdims=True), sk)
                es.append(jnp.exp2(sp - m).astype(BF16))
                es.append(jnp.exp2(sc - m).astype(BF16))
                ts.append(jnp.exp2(sk - m))
            e = jnp.concatenate(es, axis=1)
            od = jnp.dot(e, jnp.concatenate([vz, ones], axis=1), preferred_element_type=F32)
            den = od[:, LANES:] + jnp.where(lo_rows, ts[0], ts[1])
            o = od[:, :LANES] / den
            outs.extend([o[pp * ATTN_BLOCK:(pp + 1) * ATTN_BLOCK] for pp in range(4)])
        y = jnp.concatenate(outs, axis=1)
        o_ref[i * ATTN_BLOCK:(i + 1) * ATTN_BLOCK, :] = _rms_lanes(y, g_ref[...]).astype(BF16)


def _attention(q, kv, sink_cols, gain, bsz, seq, nblk):
    t = q.shape[0]
    tq = nblk * ATTN_BLOCK
    nt = seq // tq
    return pl.pallas_call(
        functools.partial(_attn_kernel, nblk=nblk),
        grid=(bsz, nt),
        in_specs=[
            pl.BlockSpec((tq, ATTN_WIDTH), lambda b, n: (b * nt + n, 0)),
            pl.BlockSpec((ATTN_BLOCK, 2 * KV_WIDTH),
                         lambda b, n: ((b * nt + n) * nblk - jnp.minimum(n, 1), 0)),
            pl.BlockSpec((tq, 2 * KV_WIDTH), lambda b, n: (b * nt + n, 0)),
            pl.BlockSpec(sink_cols.shape, lambda b, n: (0, 0, 0)),
            pl.BlockSpec((1, ATTN_WIDTH), lambda b, n: (0, 0)),
        ],
        out_specs=pl.BlockSpec((tq, ATTN_WIDTH), lambda b, n: (b * nt + n, 0)),
        out_shape=jax.ShapeDtypeStruct((t, ATTN_WIDTH), BF16),
        compiler_params=pltpu.CompilerParams(
            dimension_semantics=("parallel", "parallel"), vmem_limit_bytes=VMEM_LIMIT),
        name="swa_attention",
    )(q, kv, kv, sink_cols, gain)


def _ssm_kernel(u_ref, kfirst_ref, kodd_ref, knext_ref, pw_ref, bc_ref, cc_ref, ap_ref, y_ref,
                m_even, m_odd, *, chunks_per_seq):
    rows = u_ref.shape[2]
    tri = (lax.broadcasted_iota(jnp.int32, (LANES, LANES), 1)
           >= lax.broadcasted_iota(jnp.int32, (LANES, LANES), 0))

    def build(k_ref, m_ref):
        for hp in range(SSM_GROUP):
            for h in range(SSM_GROUP):
                k = k_ref[0, hp * SSM_GROUP + h:hp * SSM_GROUP + h + 1, :]
                kb = jnp.broadcast_to(k, (LANES, LANES))
                kr = pltpu.roll(kb, 0, 1, stride=1, stride_axis=0)
                m_ref[hp * LANES:(hp + 1) * LANES, h * LANES:(h + 1) * LANES] = (
                    jnp.where(tri, kr, 0.0).astype(BF16))

    cidx = lax.broadcasted_iota(jnp.int32, (rows, LANES), 0) & (chunks_per_seq - 1)

    def mix(slot, m_ref):
        u = jnp.concatenate([u_ref[slot, hp].astype(BF16) for hp in range(SSM_GROUP)], axis=1)
        y = jnp.dot(u, m_ref[...], preferred_element_type=F32)
        e = jnp.concatenate(
            [(pw_ref[slot, 0] * bc_ref[slot, 0, hp:hp + 1, :]
              + pw_ref[slot, 1] * bc_ref[slot, 1, hp:hp + 1, :]).astype(BF16)
             for hp in range(SSM_GROUP)], axis=0)
        s = jnp.dot(u, e, preferred_element_type=F32)
        for j in range(SSM_SCAN_STEPS):
            d = 1 << j
            prev = jnp.where(cidx >= d, pltpu.roll(s, d, axis=0), 0.0)
            a_same = ap_ref[slot, 2 * j:2 * j + 1, :]
            a_cross = ap_ref[slot, 2 * j + 1:2 * j + 2, :]
            s = s + prev * a_same + pltpu.roll(prev, SSM_STATE, axis=1) * a_cross
        s_prev = jnp.where(cidx >= 1, pltpu.roll(s, 1, axis=0), 0.0)
        f = jnp.concatenate(
            [(cc_ref[slot, :, h:h + 1] * pw_ref[slot, 2]
              + cc_ref[slot, :, SSM_GROUP + h:SSM_GROUP + h + 1] * pw_ref[slot, 3]).astype(BF16)
             for h in range(SSM_GROUP)], axis=1)
        y = y + jnp.dot(s_prev.astype(BF16), f, preferred_element_type=F32)
        for h in range(SSM_GROUP):
            y_ref[slot, h] = y[:, h * LANES:(h + 1) * LANES]

    @pl.when(pl.program_id(0) == 0)
    def _():
        build(kfirst_ref, m_even)

    mix(0, m_even)
    build(kodd_ref, m_odd)
    mix(1, m_odd)
    build(knext_ref, m_even)


def _ssm(u4, kc, pw, bc, cc, ap, layer, chunks_per_seq):
    g, hh, rows, _ = u4.shape
    kblk = (None, 1) + kc.shape[2:]
    return pl.pallas_call(
        functools.partial(_ssm_kernel, chunks_per_seq=chunks_per_seq),
        grid=(g // 2,),
        in_specs=[
            pl.BlockSpec((2, hh, rows, LANES), lambda i: (i, 0, 0, 0)),
            pl.BlockSpec(kblk, lambda i: (layer, 0, 0, 0)),
            pl.BlockSpec(kblk, lambda i: (layer, 2 * i + 1, 0, 0)),
            pl.BlockSpec(kblk, lambda i: (layer, jnp.minimum(2 * i + 2, g - 1), 0, 0)),
            pl.BlockSpec((None, 2) + pw.shape[2:], lambda i: (layer, i, 0, 0, 0)),
            pl.BlockSpec((None, 2) + bc.shape[2:], lambda i: (layer, i, 0, 0, 0)),
            pl.BlockSpec((None, 2) + cc.shape[2:], lambda i: (layer, i, 0, 0)),
            pl.BlockSpec((None, 2) + ap.shape[2:], lambda i: (layer, i, 0, 0)),
        ],
        out_specs=pl.BlockSpec((2, hh, rows, LANES), lambda i: (i, 0, 0, 0)),
        out_shape=jax.ShapeDtypeStruct(u4.shape, F32),
        scratch_shapes=[pltpu.VMEM((SSM_GROUP * LANES, SSM_GROUP * LANES), BF16),
                        pltpu.VMEM((SSM_GROUP * LANES, SSM_GROUP * LANES), BF16)],
        compiler_params=pltpu.CompilerParams(
            dimension_semantics=("arbitrary",), vmem_limit_bytes=VMEM_LIMIT),
        name="s5_chunked",
    )(u4, kc, kc, kc, pw, bc, cc, ap)


def _ssm_tables(lam_re, lam_im, log_dt, b_re, b_im, c_re, c_im, d_skip):
    hp = lax.Precision.HIGHEST
    nl = lam_re.shape[0]
    lr, li = lam_re.astype(F32), lam_im.astype(F32)
    dt = jnp.exp(log_dt.astype(F32))[..., None]
    mag = jnp.exp(lr * dt)
    ar, ai = mag * jnp.cos(li * dt), mag * jnp.sin(li * dt)
    den = lr * lr + li * li
    nre, nim = ar - 1.0, ai
    coef_re = (nre * lr + nim * li) / den
    coef_im = (nim * lr - nre * li) / den
    br, bi = b_re.astype(F32), b_im.astype(F32)
    bbr = coef_re[..., None] * br - coef_im[..., None] * bi
    bbi = coef_re[..., None] * bi + coef_im[..., None] * br
    crp = jnp.swapaxes(c_re.astype(F32), -1, -2)
    cip = jnp.swapaxes(c_im.astype(F32), -1, -2)
    n = jnp.arange(SSM_CHUNK + 1, dtype=F32)
    pmag = jnp.exp((lr * dt)[..., None] * n)
    pang = (li * dt)[..., None] * n
    pr, pi = pmag * jnp.cos(pang), pmag * jnp.sin(pang)
    wr = bbr[..., :, None] * crp[..., None, :] - bbi[..., :, None] * cip[..., None, :]
    wi = bbr[..., :, None] * cip[..., None, :] + bbi[..., :, None] * crp[..., None, :]
    taps = (jnp.einsum("lgpn,lgpab->lgabn", pr[..., :SSM_CHUNK], wr, precision=hp)
            - jnp.einsum("lgpn,lgpab->lgabn", pi[..., :SSM_CHUNK], wi, precision=hp))
    dskip = d_skip.astype(F32).reshape(nl, SSM_GROUPS, SSM_GROUP)
    lag0 = (jnp.arange(SSM_CHUNK) == 0).astype(F32)
    taps = taps + (jnp.eye(SSM_GROUP, dtype=F32) * dskip[:, :, None, :])[..., None] * lag0
    kc = taps.reshape(nl, SSM_GROUPS, SSM_GROUP * SSM_GROUP, SSM_CHUNK)
    prr = jnp.swapaxes(pr[..., :SSM_CHUNK][..., ::-1], -1, -2)
    pir = jnp.swapaxes(pi[..., :SSM_CHUNK][..., ::-1], -1, -2)
    pw = jnp.stack([jnp.concatenate([prr, prr], axis=-1),
                    jnp.concatenate([pir, pir], axis=-1),
                    jnp.concatenate([pr[..., 1:], pr[..., 1:]], axis=2),
                    jnp.concatenate([pi[..., 1:], pi[..., 1:]], axis=2)], axis=2)
    bbr_t = jnp.swapaxes(bbr, -1, -2)
    bbi_t = jnp.swapaxes(bbi, -1, -2)
    bc = jnp.stack([jnp.concatenate([bbr_t, bbi_t], axis=-1),
                    jnp.concatenate([-bbi_t, bbr_t], axis=-1)], axis=2)
    cc = jnp.concatenate([jnp.concatenate([crp, -cip], axis=2),
                          jnp.concatenate([-cip, -crp], axis=2)], axis=-1)
    a_r, a_i = pr[..., SSM_CHUNK], pi[..., SSM_CHUNK]
    rows = []
    for _ in range(SSM_SCAN_STEPS):
        rows.append(jnp.concatenate([a_r, a_r], axis=-1))
        rows.append(jnp.concatenate([-a_i, a_i], axis=-1))
        a_r, a_i = a_r * a_r - a_i * a_i, 2.0 * a_r * a_i
    rows.extend([jnp.zeros_like(rows[0])] * (16 - len(rows)))
    ap = jnp.stack(rows, axis=2)
    return kc, pw, bc, cc, ap


def _glu_kernel(y4_ref, wt_ref, g_ref, o_ref):
    def slab(c):
        return y4_ref[:, :, c, :].reshape(SSM_WIDTH, SSM_CHUNK)

    for c in range(0, y4_ref.shape[2], GLU_SLABS):
        y = _gelu_tanh(jnp.concatenate([slab(c + i) for i in range(GLU_SLABS)], axis=1))
        z = jnp.dot(wt_ref[...], y.astype(BF16), preferred_element_type=F32)
        o = y * jax.nn.sigmoid(z)
        ms = jnp.mean(o * o, axis=0, keepdims=True)
        o = o * lax.rsqrt(ms + EPS) * g_ref[...]
        o_ref[c * SSM_CHUNK:(c + GLU_SLABS) * SSM_CHUNK, :] = o.T.astype(BF16)


def _ssm_glu(y4, w_glu_t, gain_col, tm):
    t = y4.shape[2] * SSM_CHUNK
    return pl.pallas_call(
        _glu_kernel,
        grid=(t // tm,),
        in_specs=[
            pl.BlockSpec((SSM_GROUPS, SSM_GROUP, tm // SSM_CHUNK, SSM_CHUNK), lambda i: (0, 0, i, 0)),
            pl.BlockSpec((SSM_WIDTH, SSM_WIDTH), lambda i: (0, 0)),
            pl.BlockSpec((SSM_WIDTH, 1), lambda i: (0, 0)),
        ],
        out_specs=pl.BlockSpec((tm, SSM_WIDTH), lambda i: (i, 0)),
        out_shape=jax.ShapeDtypeStruct((t, SSM_WIDTH), BF16),
        compiler_params=pltpu.CompilerParams(
            dimension_semantics=("parallel",), vmem_limit_bytes=VMEM_LIMIT),
        name="s5_glu",
    )(y4, w_glu_t, gain_col)


def _out_proj_kernel(x_ref, ya_ref, ys_ref, yg_ref, w_ref, g_ref, o_ref, h_ref):
    half = x_ref.shape[0] // 2
    for rows in (slice(0, half), slice(half, 2 * half)):
        y = jnp.concatenate([ya_ref[rows, :], ys_ref[rows, :], yg_ref[rows, :]], axis=1)
        xn = x_ref[rows, :] + jnp.dot(y, w_ref[...], preferred_element_type=F32)
        o_ref[rows, :] = xn
        h_ref[rows, :] = _rms_lanes(xn, g_ref[...]).astype(BF16)


def _out_proj(x2, ya, ys, yg, w_out_bf, gain_ffn, layer, tm):
    t = x2.shape[0]
    row = lambda i: (i, 0)
    const = lambda i: (0, 0)
    return pl.pallas_call(
        _out_proj_kernel,
        grid=(t // tm,),
        in_specs=[
            pl.BlockSpec((tm, D_MODEL), row),
            pl.BlockSpec((tm, ATTN_WIDTH), row),
            pl.BlockSpec((tm, SSM_WIDTH), row),
            pl.BlockSpec((tm, GMLP_WIDTH), row),
            pl.BlockSpec((None,) + w_out_bf.shape[1:], lambda i: (layer, 0, 0),
                         pipeline_mode=pl.Buffered(1)),
            pl.BlockSpec((1, D_MODEL), const),
        ],
        out_specs=[pl.BlockSpec((tm, D_MODEL), row), pl.BlockSpec((tm, D_MODEL), row)],
        out_shape=[jax.ShapeDtypeStruct((t, D_MODEL), F32), jax.ShapeDtypeStruct((t, D_MODEL), BF16)],
        compiler_params=pltpu.CompilerParams(
            dimension_semantics=("parallel",), vmem_limit_bytes=VMEM_LIMIT),
        name="out_proj",
    )(x2, ya, ys, yg, w_out_bf, gain_ffn)


FFN_RES_CHUNK = 256
FFN_RES_STEPS = D_MODEL // FFN_RES_CHUNK


def _ffn_kernel(h_ref, xc_ref, wg_ref, wu_ref, wd_ref, gf_ref, o_ref, *, final_norm, halves):
    j = pl.program_id(1)

    @pl.when(j == 0)
    def _():
        o_ref[...] = jnp.zeros_like(o_ref)

    @pl.when(j < FFN_RES_STEPS)
    def _():
        cols = pl.ds(pl.multiple_of(j * FFN_RES_CHUNK, FFN_RES_CHUNK), FFN_RES_CHUNK)
        o_ref[:, cols] += xc_ref[...]

    h = h_ref[...]
    hw = wg_ref.shape[1] // halves
    acc = None
    for c in range(halves):
        gate = jnp.dot(h, wg_ref[:, c * hw:(c + 1) * hw], preferred_element_type=F32)
        up = jnp.dot(h, wu_ref[:, c * hw:(c + 1) * hw], preferred_element_type=F32)
        a = (gate * jax.nn.sigmoid(gate) * up).astype(BF16)
        d = jnp.dot(a, wd_ref[c * hw:(c + 1) * hw, :], preferred_element_type=F32)
        acc = d if acc is None else acc + d
    o_ref[...] += acc

    if final_norm:
        @pl.when(j == pl.num_programs(1) - 1)
        def _():
            o_ref[...] = _rms_lanes(o_ref[...], gf_ref[...])


def _ffn(h, x2, wg_bf, wu_bf, wd_bf, gain_final, layer, tm, tf, final_norm):
    t = x2.shape[0]
    d_ff = wg_bf.shape[2]
    assert d_ff // tf >= FFN_RES_STEPS
    return pl.pallas_call(
        functools.partial(_ffn_kernel, final_norm=final_norm, halves=2),
        grid=(t // tm, d_ff // tf),
        in_specs=[
            pl.BlockSpec((tm, D_MODEL), lambda i, j: (i, 0)),
            pl.BlockSpec((tm, FFN_RES_CHUNK), lambda i, j: (i, jnp.minimum(j, FFN_RES_STEPS - 1))),
            pl.BlockSpec((None, D_MODEL, tf), lambda i, j: (layer, 0, j)),
            pl.BlockSpec((None, D_MODEL, tf), lambda i, j: (layer, 0, j)),
            pl.BlockSpec((None, tf, D_MODEL), lambda i, j: (layer, j, 0)),
            pl.BlockSpec((1, D_MODEL), lambda i, j: (0, 0)),
        ],
        out_specs=pl.BlockSpec((tm, D_MODEL), lambda i, j: (i, 0)),
        out_shape=jax.ShapeDtypeStruct((t, D_MODEL), F32),
        compiler_params=pltpu.CompilerParams(
            dimension_semantics=("parallel", "arbitrary"), vmem_limit_bytes=VMEM_LIMIT),
        name="ffn",
    )(h, x2, wg_bf, wu_bf, wd_bf, gain_final)


def kernel(x, norm_mix, w_in, attn_sinks, ssm_lam_re, ssm_lam_im, ssm_log_dt, ssm_b_re, ssm_b_im,
           ssm_c_re, ssm_c_im, ssm_d, ssm_w_glu, gmlp_ln_g, gmlp_ln_b, gmlp_w_s, gmlp_b_s,
           out_norm_attn, out_norm_ssm, out_norm_gmlp, w_out, norm_ffn, w_gate, w_up, w_down,
           norm_final):
    bsz, seq, _ = x.shape
    depth = w_in.shape[0]
    t = bsz * seq
    chunks_per_seq = seq // SSM_CHUNK
    assert seq % 2048 == 0 and (1 << SSM_SCAN_STEPS) == chunks_per_seq

    w_in_bf = _to_bf16(w_in, rows=256)
    w_out_bf = _to_bf16(w_out, rows=256)
    w_gate_bf = _to_bf16(w_gate, rows=256)
    w_up_bf = _to_bf16(w_up, rows=256)
    w_down_bf = _to_bf16(w_down, rows=512)
    ssm_tbls = _ssm_tables(ssm_lam_re, ssm_lam_im, ssm_log_dt, ssm_b_re, ssm_b_im,
                           ssm_c_re, ssm_c_im, ssm_d)

    causal = jnp.tril(jnp.ones((GMLP_CHUNK, GMLP_CHUNK), dtype=bool))
    head_of = ((jnp.arange(N_KV_HEADS)[None, :, None] * 4 + (jnp.arange(4 * ATTN_BLOCK) // ATTN_BLOCK)[:, None, None]) * 2
               + jnp.arange(2)[None, None, :]).reshape(4 * ATTN_BLOCK, 2 * N_KV_HEADS)

    x2 = x.reshape(t, D_MODEL)
    for l in range(depth):
        wst = _ssm_weight_t(w_in, l, cols=256)
        ws = jnp.where(causal[None], gmlp_w_s[l], 0.0).astype(BF16)
        q, kv, y_gmlp, u4 = _in_proj(x2, norm_mix[l][None], w_in_bf, wst, gmlp_ln_g[l][None],
                                     gmlp_ln_b[l][None], ws, gmlp_b_s[l].T, out_norm_gmlp[l][None],
                                     l, tm=1024)

        sink_cols = jnp.broadcast_to(
            (attn_sinks[l].astype(F32) * LOG2E)[head_of].T[:, :, None], (2 * N_KV_HEADS, 4 * ATTN_BLOCK, LANES))
        y_attn = _attention(q, kv, sink_cols, out_norm_attn[l][None], bsz, seq, nblk=16)

        y4 = _ssm(u4, *ssm_tbls, l, chunks_per_seq)
        y_ssm = _ssm_glu(y4, ssm_w_glu[l].T.astype(BF16), out_norm_ssm[l][:, None], tm=2048)

        x2, h = _out_proj(x2, y_attn, y_ssm, y_gmlp, w_out_bf, norm_ffn[l][None], l, tm=512)
        x2 = _ffn(h, x2, w_gate_bf, w_up_bf, w_down_bf, norm_final[None], l, tm=1024, tf=512,
                  final_norm=(l == depth - 1))
    return x2.reshape(bsz, seq, D_MODEL)
```
